```python
import jax, jax.numpy as jnp
from jax import lax
import numpy as np

D_MODEL = 1024
BATCH = 4
SEQ = 8192
DEPTH = 1

CHUNK = 64
Q_BLOCK = 128
SB_HEADS = 8
SB_HEAD_DIM = D_MODEL // SB_HEADS
SB_WIDTH = SB_HEADS * SB_HEAD_DIM
RET_HEADS = 4
RET_KEY_DIM = D_MODEL // RET_HEADS
RET_VAL_DIM = D_MODEL // RET_HEADS
RET_QK_WIDTH = RET_HEADS * RET_KEY_DIM
RET_WIDTH = RET_HEADS * RET_VAL_DIM
ROPE_BASE = 10000.0
EPS = 1e-6
SPLIT_SIZES = (SB_WIDTH, SB_WIDTH, SB_WIDTH, SB_WIDTH,
               RET_QK_WIDTH, RET_QK_WIDTH, RET_WIDTH, RET_WIDTH,
               D_MODEL, D_MODEL)
IN_COLS = sum(SPLIT_SIZES)
SPLIT_POINTS = tuple(int(v) for v in np.cumsum(SPLIT_SIZES)[:-1])

kernel_name = "hybrid_stickbreak_retention_gated_block"


def rms_norm(x, gain):
    xf = x.astype(jnp.float32)
    y = xf * lax.rsqrt(jnp.mean(xf * xf, axis=-1, keepdims=True) + EPS)
    return (y * gain.astype(jnp.float32)).astype(x.dtype)


def split_heads(t, n_heads):
    b, s, w = t.shape
    return t.reshape(b, s, n_heads, w // n_heads).transpose(0, 2, 1, 3)


def merge_heads(t):
    b, h, s, d = t.shape
    return t.transpose(0, 2, 1, 3).reshape(b, s, h * d)


def rotary(t):
    s, d = t.shape[2], t.shape[3]
    inv_freq = ROPE_BASE ** (-jnp.arange(0, d, 2, dtype=jnp.float32) / d)
    ang = jnp.arange(s, dtype=jnp.float32)[:, None] * inv_freq[None, :]
    cos, sin = jnp.cos(ang), jnp.sin(ang)
    tf = t.astype(jnp.float32)
    t1, t2 = tf[..., : d // 2], tf[..., d // 2:]
    out = jnp.concatenate([t1 * cos - t2 * sin, t1 * sin + t2 * cos], axis=-1)
    return out.astype(t.dtype)


def stick_breaking_attention(q, k, v):
    b, h, s, d = q.shape
    nb = s // Q_BLOCK
    qb = q.reshape(b, h, nb, Q_BLOCK, d).transpose(2, 0, 1, 3, 4)
    kpos = jnp.arange(s)
    scale = d ** -0.5

    def block(args):
        qi, i = args
        z = jnp.einsum('bhqd,bhkd->bhqk', qi, k).astype(jnp.float32) * scale
        qpos = i * Q_BLOCK + jnp.arange(Q_BLOCK)
        mask = kpos[None, :] < qpos[:, None]
        log_keep = jnp.where(mask, jax.nn.log_sigmoid(-z), 0.0)
        suffix = lax.cumsum(log_keep, axis=3, reverse=True) - log_keep
        w = jnp.where(mask, jnp.exp(jax.nn.log_sigmoid(z) + suffix), 0.0)
        return jnp.einsum('bhqk,bhkd->bhqd', w.astype(v.dtype), v)

    out = lax.map(block, (qb, jnp.arange(nb)))
    return out.transpose(1, 2, 0, 3, 4).reshape(b, h, s, d)


def chunkwise_retention(q, k, v):
    dtype = v.dtype
    q, k, v = (a.astype(jnp.float32) for a in (q, k, v))
    b, h, s, dk = q.shape
    dv = v.shape[-1]
    n = s // CHUNK
    log_gamma = jnp.log1p(-jnp.exp2(-5.0 - jnp.arange(h, dtype=jnp.float32)))
    qc = q.reshape(b, h, n, CHUNK, dk)
    kc = k.reshape(b, h, n, CHUNK, dk)
    vc = v.reshape(b, h, n, CHUNK, dv)
    idx = jnp.arange(CHUNK, dtype=jnp.float32)
    intra_decay = jnp.exp(log_gamma[:, None, None] * jnp.abs(idx[:, None] - idx[None, :]))
    scores = jnp.einsum('bhnqd,bhnkd->bhnqk', qc, kc) * intra_decay[None, :, None]
    intra = jnp.einsum('bhnqk,bhnke->bhnqe', scores, vc)
    q_decay = jnp.exp(log_gamma[:, None] * (idx + 1.0))[None, :, :, None]
    k_decay = jnp.exp(log_gamma[:, None] * (CHUNK - 1.0 - idx))[None, :, :, None]
    chunk_decay = jnp.exp(log_gamma * CHUNK)[None, :, None, None]

    def step(state, inp):
        qi, ki, vi = inp
        out = jnp.einsum('bhqd,bhde->bhqe', qi * q_decay, state)
        state = state * chunk_decay + jnp.einsum('bhkd,bhke->bhde', ki * k_decay, vi)
        return state, out

    init = jnp.zeros((b, h, dk, dv), jnp.float32)
    xs = (qc.transpose(2, 0, 1, 3, 4), kc.transpose(2, 0, 1, 3, 4), vc.transpose(2, 0, 1, 3, 4))
    _, inter = lax.scan(step, init, xs)
    out = intra + inter.transpose(1, 2, 0, 3, 4)
    return out.reshape(b, h, s, dv).astype(dtype)


def setup_inputs(seed: int = 0) -> dict:
    key = jax.random.key(seed)
    ks = jax.random.split(key, 10)
    f32 = jnp.float32
    x = jax.random.normal(ks[0], (BATCH, SEQ, D_MODEL), f32)
    norm_gain = 1.0 + 0.05 * jax.random.normal(ks[1], (DEPTH, D_MODEL), f32)
    w_in = jax.random.normal(ks[2], (DEPTH, D_MODEL, IN_COLS), f32) * D_MODEL ** -0.5
    b_merge = 0.01 * jax.random.normal(ks[3], (DEPTH, 2, D_MODEL), f32)
    sb_q_gain = 1.0 + 0.05 * jax.random.normal(ks[4], (DEPTH, SB_HEAD_DIM), f32)
    sb_k_gain = 1.0 + 0.05 * jax.random.normal(ks[5], (DEPTH, SB_HEAD_DIM), f32)
    ret_out_gain = 1.0 + 0.05 * jax.random.normal(ks[6], (DEPTH, RET_HEADS, RET_VAL_DIM), f32)
    w_branch_sb = jax.random.normal(ks[7], (DEPTH, SB_WIDTH, D_MODEL), f32) * SB_WIDTH ** -0.5
    w_branch_ret = jax.random.normal(ks[8], (DEPTH, RET_WIDTH, D_MODEL), f32) * RET_WIDTH ** -0.5
    w_out = jax.random.normal(ks[9], (DEPTH, D_MODEL, D_MODEL), f32) * D_MODEL ** -0.5
    return {"x": x, "norm_gain": norm_gain, "w_in": w_in, "b_merge": b_merge,
            "sb_q_gain": sb_q_gain, "sb_k_gain": sb_k_gain, "ret_out_gain": ret_out_gain,
            "w_branch_sb": w_branch_sb, "w_branch_ret": w_branch_ret, "w_out": w_out}


def reference(x, norm_gain, w_in, b_merge, sb_q_gain, sb_k_gain, ret_out_gain,
              w_branch_sb, w_branch_ret, w_out):
    for layer in range(DEPTH):
        h = rms_norm(x, norm_gain[layer])
        proj = jnp.einsum('bsd,dc->bsc', h, w_in[layer])
        (sb_q, sb_k, sb_v, sb_g, r_q, r_k, r_v, r_g, m_sb, m_ret) = jnp.split(proj, SPLIT_POINTS, axis=-1)

        qa = rms_norm(split_heads(sb_q, SB_HEADS), sb_q_gain[layer])
        ka = rms_norm(split_heads(sb_k, SB_HEADS), sb_k_gain[layer])
        va = split_heads(sb_v, SB_HEADS)
        out_a = merge_heads(stick_breaking_attention(qa, ka, va)) * jax.nn.silu(sb_g)

        qb = rotary(split_heads(r_q, RET_HEADS))
        kb = rotary(split_heads(r_k, RET_HEADS)) * (RET_KEY_DIM ** -0.5)
        vb = split_heads(r_v, RET_HEADS)
        ret = chunkwise_retention(qb, kb, vb)
        ret = rms_norm(ret, ret_out_gain[layer][None, :, None, :])
        out_b = merge_heads(ret) * jax.nn.silu(r_g)

        p_a = jnp.einsum('bsw,wd->bsd', out_a, w_branch_sb[layer])
        p_b = jnp.einsum('bsw,wd->bsd', out_b, w_branch_ret[layer])
        merged = (jax.nn.sigmoid(m_sb + b_merge[layer, 0]) * p_a
                  + jax.nn.sigmoid(m_ret + b_merge[layer, 1]) * p_b)
        x = x + jnp.einsum('bsd,de->bse', merged, w_out[layer])
    return x
```

```python
import functools
import math

import jax
import jax.numpy as jnp
from jax import lax
from jax.experimental import pallas as pl
from jax.experimental.pallas import tpu as pltpu

D_MODEL = 1024
SB_HEADS = 8
SB_HEAD_DIM = D_MODEL // SB_HEADS
RET_HEADS = 4
RET_DIM = D_MODEL // RET_HEADS
CHUNK = 64
ROPE_BASE = 10000.0
EPS = 1e-6
N_SEGMENTS = 10
LOG2E = 1.4426950408889634

VMEM_LIMIT_BYTES = 56 * 1024 * 1024

IN_TM = 256
SB_QB = 512
SB_KB = 256
RET_L = 256
RET_T = 1024
OUT_TM = 512

F32 = jnp.float32
BF16 = jnp.bfloat16


def _rms(x, gain):
    y = x * lax.rsqrt(jnp.mean(x * x, axis=-1, keepdims=True) + EPS)
    return y * gain


def _inproj_kernel(x_ref, ng_ref, w_ref, qg_ref, kg_ref, cos_ref, sin_ref,
                   q_ref, k_ref, v_ref, sg_ref, rq_ref, rk_ref, rv_ref, rg_ref,
                   msb_ref, mret_ref):
    hb = _rms(x_ref[...], ng_ref[...]).astype(BF16)

    def seg(s):
        return jnp.dot(hb, w_ref[:, s * D_MODEL:(s + 1) * D_MODEL],
                       preferred_element_type=F32)

    def head_norm(p, gain_ref, out_ref):
        for h in range(SB_HEADS):
            sl = slice(h * SB_HEAD_DIM, (h + 1) * SB_HEAD_DIM)
            out_ref[:, sl] = _rms(p[:, sl], gain_ref[...]).astype(BF16)

    def rotary(p, out_ref, post_scale):
        cos = cos_ref[...]
        sin = sin_ref[...]
        half = RET_DIM // 2
        for h in range(RET_HEADS):
            t1 = p[:, h * RET_DIM:h * RET_DIM + half]
            t2 = p[:, h * RET_DIM + half:(h + 1) * RET_DIM]
            o1 = t1 * cos - t2 * sin
            o2 = t1 * sin + t2 * cos
            if post_scale is not None:
                o1 = o1 * post_scale
                o2 = o2 * post_scale
            out_ref[:, h * RET_DIM:h * RET_DIM + half] = o1.astype(BF16)
            out_ref[:, h * RET_DIM + half:(h + 1) * RET_DIM] = o2.astype(BF16)

    head_norm(seg(0), qg_ref, q_ref)
    head_norm(seg(1), kg_ref, k_ref)
    v_ref[...] = seg(2).astype(BF16)
    sg_ref[...] = seg(3)
    rotary(seg(4), rq_ref, None)
    rotary(seg(5), rk_ref, RET_DIM ** -0.5)
    rv_ref[...] = seg(6).astype(BF16)
    rg_ref[...] = seg(7)
    msb_ref[...] = seg(8)
    mret_ref[...] = seg(9)


def _inproj(x2, norm_gain, w_in_bf, q_gain, k_gain, cos, sin, seq):
    m = x2.shape[0]
    tm = IN_TM
    pos_blocks = seq // tm
    row = lambda i: (i, 0)
    const = lambda i: (0, 0)
    pos = lambda i: (i % pos_blocks, 0)
    bf_out = jax.ShapeDtypeStruct((m, D_MODEL), BF16)
    f32_out = jax.ShapeDtypeStruct((m, D_MODEL), F32)
    out_spec = pl.BlockSpec((tm, D_MODEL), row)
    return pl.pallas_call(
        _inproj_kernel,
        grid=(m // tm,),
        in_specs=[
            pl.BlockSpec((tm, D_MODEL), row),
            pl.BlockSpec((1, D_MODEL), const),
            pl.BlockSpec((D_MODEL, N_SEGMENTS * D_MODEL), const),
            pl.BlockSpec((1, SB_HEAD_DIM), const),
            pl.BlockSpec((1, SB_HEAD_DIM), const),
            pl.BlockSpec((tm, RET_DIM // 2), pos),
            pl.BlockSpec((tm, RET_DIM // 2), pos),
        ],
        out_specs=[out_spec] * 10,
        out_shape=[bf_out, bf_out, bf_out, f32_out, bf_out, bf_out, bf_out,
                   f32_out, f32_out, f32_out],
        compiler_params=pltpu.CompilerParams(
            dimension_semantics=("arbitrary",),
            vmem_limit_bytes=VMEM_LIMIT_BYTES),
        name="inproj",
    )(x2, norm_gain, w_in_bf, q_gain, k_gain, cos, sin)


def _sb_kernel(q_ref, k_ref, v_ref, g_ref, tri_ref, o_ref, vt_ref, acc_ref, *,
               seq, qb, kb):
    i = pl.program_id(2)
    zscale = (SB_HEAD_DIM ** -0.5) * LOG2E
    diag_tiles = qb // kb

    @pl.when(i == 0)
    def _():
        step = 512
        for c in range(seq // step):
            vt_ref[:, c * step:(c + 1) * step] = (
                v_ref[c * step:(c + 1) * step, :].astype(F32).T.astype(BF16))

    qt = q_ref[...].astype(F32).T.astype(BF16)
    tri = tri_ref[...]
    acc_ref[...] = jnp.zeros_like(acc_ref)

    def tile(kstart, carry, mask):
        kt = k_ref[pl.ds(kstart, kb), :]
        z2 = jnp.dot(kt, qt, preferred_element_type=F32) * zscale
        sp2 = jnp.maximum(z2, 0.0) + jnp.log(1.0 + jnp.exp2(-jnp.abs(z2))) * LOG2E
        if mask is not None:
            sp2 = jnp.where(mask, sp2, 0.0)
        hi = sp2.astype(BF16)
        lo = (sp2 - hi.astype(F32)).astype(BF16)
        cs = (jnp.dot(tri, hi, preferred_element_type=F32)
              + jnp.dot(tri, lo, preferred_element_type=F32))
        w = jnp.exp2(z2 - cs - carry)
        if mask is not None:
            w = jnp.where(mask, w, 0.0)
        vt = vt_ref[:, pl.ds(kstart, kb)]
        acc_ref[...] += jnp.dot(vt, w.astype(BF16), preferred_element_type=F32)
        return carry + cs[0:1, :]

    carry = jnp.zeros((1, qb), F32)
    qbase = i * qb
    row = lax.broadcasted_iota(jnp.int32, (kb, qb), 0)
    col = lax.broadcasted_iota(jnp.int32, (kb, qb), 1)
    for c in reversed(range(diag_tiles)):
        mask = (row + c * kb) < col
        carry = tile(pl.multiple_of(qbase + c * kb, kb), carry, mask)

    def body(jj, carry):
        kstart = pl.multiple_of(qbase - (jj + 1) * kb, kb)
        return tile(kstart, carry, None)

    lax.fori_loop(0, i * diag_tiles, body, carry)

    g = g_ref[...]
    out = acc_ref[...].T * (g * jax.nn.sigmoid(g))
    o_ref[...] = out.astype(BF16)


def _sb_attention(q, k, v, g, batch, seq):
    qb, kb = SB_QB, SB_KB
    nq = seq // qb
    tri = (jnp.arange(kb)[None, :] >= jnp.arange(kb)[:, None]).astype(BF16)
    qmap = lambda b, h, i: (b * nq + i, h)
    kvmap = lambda b, h, i: (b, h)
    return pl.pallas_call(
        functools.partial(_sb_kernel, seq=seq, qb=qb, kb=kb),
        grid=(batch, SB_HEADS, nq),
        in_specs=[
            pl.BlockSpec((qb, SB_HEAD_DIM), qmap),
            pl.BlockSpec((seq, SB_HEAD_DIM), kvmap),
            pl.BlockSpec((seq, SB_HEAD_DIM), kvmap),
            pl.BlockSpec((qb, SB_HEAD_DIM), qmap),
            pl.BlockSpec((kb, kb), lambda b, h, i: (0, 0)),
        ],
        out_specs=pl.BlockSpec((qb, SB_HEAD_DIM), qmap),
        out_shape=jax.ShapeDtypeStruct((batch * seq, D_MODEL), BF16),
        scratch_shapes=[
            pltpu.VMEM((SB_HEAD_DIM, seq), BF16),
            pltpu.VMEM((SB_HEAD_DIM, qb), F32),
        ],
        compiler_params=pltpu.CompilerParams(
            dimension_semantics=("arbitrary", "arbitrary", "arbitrary"),
            vmem_limit_bytes=VMEM_LIMIT_BYTES),
        name="stickbreak",
    )(q, k, v, g, tri)


def _ret_kernel(q_ref, k_ref, v_ref, g_ref, d_ref, qd_ref, kd_ref, gl_ref, og_ref,
                o_ref, state_ref, *, tokens, blk):
    n = pl.program_id(2)

    @pl.when(n == 0)
    def _():
        state_ref[...] = jnp.zeros_like(state_ref)

    decay = d_ref[...]
    qdec = qd_ref[...]
    kdec = kd_ref[...]
    gl = gl_ref[...]
    gain = og_ref[...]
    for c in range(tokens // blk):
        sl = slice(c * blk, (c + 1) * blk)
        q = q_ref[sl, :]
        k = k_ref[sl, :]
        v = v_ref[sl, :]
        s = lax.dot_general(q, k, (((1,), (1,)), ((), ())), preferred_element_type=F32)
        o = jnp.dot((s * decay).astype(BF16), v, preferred_element_type=F32)
        state = state_ref[...]
        o = o + qdec * jnp.dot(q, state.astype(BF16), preferred_element_type=F32)
        kdt = (k.astype(F32) * kdec).T.astype(BF16)
        state_ref[...] = state * gl + jnp.dot(kdt, v, preferred_element_type=F32)
        g = g_ref[sl, :]
        o_ref[sl, :] = (_rms(o, gain) * (g * jax.nn.sigmoid(g))).astype(BF16)


def _retention(rq, rk, rv, rg, ret_out_gain, batch, seq):
    blk = RET_L
    tokens = min(RET_T, seq)
    nt = seq // tokens
    log_gamma = jnp.log1p(-jnp.exp2(-5.0 - jnp.arange(RET_HEADS, dtype=F32)))
    t = jnp.arange(blk, dtype=F32)
    ct = jnp.arange(blk) // CHUNK
    dist = t[:, None] - t[None, :]
    lg = log_gamma[:, None, None]
    same = (ct[:, None] == ct[None, :])[None]
    earlier = (ct[None, :] < ct[:, None])[None]
    decay = jnp.where(same, jnp.exp(lg * jnp.abs(dist)[None]),
                      jnp.where(earlier, jnp.exp(lg * dist[None]), 0.0))
    ones = jnp.ones((1, 1, RET_DIM), F32)
    qdec = jnp.exp(log_gamma[:, None] * (t + 1.0)[None, :])[:, :, None] * ones
    kdec = jnp.exp(log_gamma[:, None] * (blk - 1.0 - t)[None, :])[:, :, None] * ones
    gl = jnp.exp(log_gamma * blk)[:, None, None] * ones
    gain = ret_out_gain.reshape(RET_HEADS, 1, RET_DIM)

    tmap = lambda b, h, n: (b * nt + n, h)
    hmap = lambda b, h, n: (h, 0, 0)
    tok_spec = pl.BlockSpec((tokens, RET_DIM), tmap)
    return pl.pallas_call(
        functools.partial(_ret_kernel, tokens=tokens, blk=blk),
        grid=(batch, RET_HEADS, nt),
        in_specs=[
            tok_spec, tok_spec, tok_spec, tok_spec,
            pl.BlockSpec((None, blk, blk), hmap),
            pl.BlockSpec((None, blk, RET_DIM), hmap),
            pl.BlockSpec((None, blk, RET_DIM), hmap),
            pl.BlockSpec((None, 1, RET_DIM), hmap),
            pl.BlockSpec((None, 1, RET_DIM), hmap),
        ],
        out_specs=tok_spec,
        out_shape=jax.ShapeDtypeStruct((batch * seq, D_MODEL), BF16),
        scratch_shapes=[pltpu.VMEM((RET_DIM, RET_DIM), F32)],
        compiler_params=pltpu.CompilerParams(
            dimension_semantics=("arbitrary", "arbitrary", "arbitrary"),
            vmem_limit_bytes=VMEM_LIMIT_BYTES),
        name="retention",
    )(rq, rk, rv, rg, decay, qdec, kdec, gl, gain)


def _out_kernel(oa_ref, ob_ref, msb_ref, mret_ref, bm_ref, x_ref, wsb_ref, wret_ref,
                wout_ref, o_ref):
    pa = jnp.dot(oa_ref[...], wsb_ref[...], preferred_element_type=F32)
    pb = jnp.dot(ob_ref[...], wret_ref[...], preferred_element_type=F32)
    merged = (jax.nn.sigmoid(msb_ref[...] + bm_ref[0:1, :]) * pa
              + jax.nn.sigmoid(mret_ref[...] + bm_ref[1:2, :]) * pb)
    o_ref[...] = x_ref[...] + jnp.dot(merged.astype(BF16), wout_ref[...],
                                      preferred_element_type=F32)


def _out_proj(oa, ob, msb, mret, b_merge, x2, wsb, wret, wout):
    m = x2.shape[0]
    tm = OUT_TM
    row = lambda i: (i, 0)
    const = lambda i: (0, 0)
    act = pl.BlockSpec((tm, D_MODEL), row)
    wspec = pl.BlockSpec((D_MODEL, D_MODEL), const)
    return pl.pallas_call(
        _out_kernel,
        grid=(m // tm,),
        in_specs=[act, act, act, act, pl.BlockSpec((2, D_MODEL), const), act,
                  wspec, wspec, wspec],
        out_specs=act,
        out_shape=jax.ShapeDtypeStruct((m, D_MODEL), F32),
        compiler_params=pltpu.CompilerParams(
            dimension_semantics=("arbitrary",),
            vmem_limit_bytes=VMEM_LIMIT_BYTES),
        name="outproj",
    )(oa, ob, msb, mret, b_merge, x2, wsb, wret, wout)


def _rope_tables(seq):
    d = RET_DIM
    inv_freq = ROPE_BASE ** (-jnp.arange(0, d, 2, dtype=F32) / d)
    ang = jnp.arange(seq, dtype=F32)[:, None] * inv_freq[None, :]
    return jnp.cos(ang), jnp.sin(ang)


def kernel(x, norm_gain, w_in, b_merge, sb_q_gain, sb_k_gain, ret_out_gain,
           w_branch_sb, w_branch_ret, w_out):
    batch, seq, d_model = x.shape
    depth = norm_gain.shape[0]
    assert d_model == D_MODEL and w_in.shape[-1] == N_SEGMENTS * D_MODEL
    assert seq % max(SB_QB, IN_TM, OUT_TM, RET_L) == 0 and seq % min(RET_T, seq) == 0
    cos, sin = _rope_tables(seq)
    x2 = x.reshape(batch * seq, D_MODEL)
    for layer in range(depth):
        q, k, v, sg, rq, rk, rv, rg, msb, mret = _inproj(
            x2, norm_gain[layer][None, :], w_in[layer].astype(BF16),
            sb_q_gain[layer][None, :], sb_k_gain[layer][None, :], cos, sin, seq)
        out_a = _sb_attention(q, k, v, sg, batch, seq)
        out_b = _retention(rq, rk, rv, rg, ret_out_gain[layer], batch, seq)
        x2 = _out_proj(out_a, out_b, msb, mret, b_merge[layer], x2,
                       w_branch_sb[layer].astype(BF16), w_branch_ret[layer].astype(BF16),
                       w_out[layer].astype(BF16))
    return x2.reshape(batch, seq, D_MODEL)
```

```python
import functools
import math

import jax
import jax.numpy as jnp
from jax import lax
from jax.experimental import pallas as pl
from jax.experimental.pallas import tpu as pltpu

D_MODEL = 1024
SB_HEADS = 8
SB_HEAD_DIM = D_MODEL // SB_HEADS
RET_HEADS = 4
RET_DIM = D_MODEL // RET_HEADS
CHUNK = 64
ROPE_BASE = 10000.0
EPS = 1e-6
N_SEGMENTS = 10
LOG2E = 1.4426950408889634

VMEM_LIMIT_BYTES = 56 * 1024 * 1024

IN_TM = 256
SB_QB = 1024
SB_KB = 256
SB_STREAM = 1024
RET_L = 256
RET_T = 1024
OUT_TM = 512

F32 = jnp.float32
BF16 = jnp.bfloat16


def _rms(x, gain):
    y = x * lax.rsqrt(jnp.mean(x * x, axis=-1, keepdims=True) + EPS)
    return y * gain


def _inproj_kernel(x_ref, ng_ref, w_ref, qg_ref, kg_ref, cos_ref, sin_ref,
                   q_ref, k_ref, v_ref, sg_ref, rq_ref, rk_ref, rv_ref, rg_ref,
                   msb_ref, mret_ref):
    hb = _rms(x_ref[...], ng_ref[...]).astype(BF16)

    def seg(s):
        return jnp.dot(hb, w_ref[:, s * D_MODEL:(s + 1) * D_MODEL],
                       preferred_element_type=F32)

    def head_norm(p, gain_ref, out_ref, post_scale=None):
        for h in range(SB_HEADS):
            sl = slice(h * SB_HEAD_DIM, (h + 1) * SB_HEAD_DIM)
            y = _rms(p[:, sl], gain_ref[...])
            if post_scale is not None:
                y = y * post_scale
            out_ref[:, sl] = y.astype(BF16)

    def rotary(p, out_ref, post_scale):
        cos = cos_ref[...]
        sin = sin_ref[...]
        half = RET_DIM // 2
        for h in range(RET_HEADS):
            t1 = p[:, h * RET_DIM:h * RET_DIM + half]
            t2 = p[:, h * RET_DIM + half:(h + 1) * RET_DIM]
            o1 = t1 * cos - t2 * sin
            o2 = t1 * sin + t2 * cos
            if post_scale is not None:
                o1 = o1 * post_scale
                o2 = o2 * post_scale
            out_ref[:, h * RET_DIM:h * RET_DIM + half] = o1.astype(BF16)
            out_ref[:, h * RET_DIM + half:(h + 1) * RET_DIM] = o2.astype(BF16)

    head_norm(seg(0), qg_ref, q_ref, (SB_HEAD_DIM ** -0.5) * LOG2E)
    head_norm(seg(1), kg_ref, k_ref)
    v_ref[...] = seg(2).astype(BF16)
    sg_ref[...] = seg(3)
    rotary(seg(4), rq_ref, None)
    rotary(seg(5), rk_ref, RET_DIM ** -0.5)
    rv_ref[...] = seg(6).astype(BF16)
    rg_ref[...] = seg(7)
    msb_ref[...] = seg(8)
    mret_ref[...] = seg(9)


def _inproj(x2, norm_gain, w_in_bf, q_gain, k_gain, cos, sin, seq):
    m = x2.shape[0]
    tm = IN_TM
    pos_blocks = seq // tm
    row = lambda i: (i, 0)
    const = lambda i: (0, 0)
    pos = lambda i: (i % pos_blocks, 0)
    bf_out = jax.ShapeDtypeStruct((m, D_MODEL), BF16)
    f32_out = jax.ShapeDtypeStruct((m, D_MODEL), F32)
    out_spec = pl.BlockSpec((tm, D_MODEL), row)
    return pl.pallas_call(
        _inproj_kernel,
        grid=(m // tm,),
        in_specs=[
            pl.BlockSpec((tm, D_MODEL), row),
            pl.BlockSpec((1, D_MODEL), const),
            pl.BlockSpec((D_MODEL, N_SEGMENTS * D_MODEL), const),
            pl.BlockSpec((1, SB_HEAD_DIM), const),
            pl.BlockSpec((1, SB_HEAD_DIM), const),
            pl.BlockSpec((tm, RET_DIM // 2), pos),
            pl.BlockSpec((tm, RET_DIM // 2), pos),
        ],
        out_specs=[out_spec] * 10,
        out_shape=[bf_out, bf_out, bf_out, f32_out, bf_out, bf_out, bf_out,
                   f32_out, f32_out, f32_out],
        compiler_params=pltpu.CompilerParams(
            dimension_semantics=("arbitrary",),
            vmem_limit_bytes=VMEM_LIMIT_BYTES),
        name="inproj",
    )(x2, norm_gain, w_in_bf, q_gain, k_gain, cos, sin)


def _sb_kernel(q_ref, k_ref, v_ref, g_ref, tri_ref, o_ref, vt_ref, qt_ref, acc_ref,
               carry_ref, *, seq, qb, kb, stream):
    i = pl.program_id(2)
    nsub = qb // kb

    @pl.when(i == 0)
    def _():
        step = 512
        for c in range(seq // step):
            vt_ref[:, c * step:(c + 1) * step] = (
                v_ref[c * step:(c + 1) * step, :].astype(F32).T.astype(BF16))

    qt_ref[...] = q_ref[...].astype(F32).T.astype(BF16)
    tri = tri_ref[...]
    acc_ref[...] = jnp.zeros_like(acc_ref)
    carry_ref[...] = jnp.zeros_like(carry_ref)

    def unit(kstart, col0, ncols, masked):
        cols = slice(col0, col0 + ncols)
        kt = k_ref[pl.ds(kstart, kb), :]
        z2 = jnp.dot(kt, qt_ref[:, cols], preferred_element_type=F32)
        sp2 = jnp.maximum(z2, 0.0) + jnp.log(1.0 + jnp.exp2(-jnp.abs(z2))) * LOG2E
        if masked:
            row = lax.broadcasted_iota(jnp.int32, z2.shape, 0)
            col = lax.broadcasted_iota(jnp.int32, z2.shape, 1)
            mask = row < col
            sp2 = jnp.where(mask, sp2, 0.0)
        cs = jnp.dot(tri, sp2.astype(BF16), preferred_element_type=F32)
        carry = carry_ref[:, cols]
        w = jnp.exp2(z2 - cs - carry)
        if masked:
            w = jnp.where(mask, w, 0.0)
        carry_ref[:, cols] = carry + cs[0:1, :]
        return w.astype(BF16)

    qbase = i * qb

    for c in reversed(range(nsub)):
        kstart = pl.multiple_of(qbase + c * kb, kb)
        vt = vt_ref[:, pl.ds(kstart, kb)]
        pieces = [(c * kb, kb, True)]
        col0 = (c + 1) * kb
        while col0 < qb:
            ncols = min(stream, qb - col0)
            pieces.append((col0, ncols, False))
            col0 += ncols
        for col0, ncols, masked in pieces:
            w = unit(kstart, col0, ncols, masked)
            acc_ref[:, col0:col0 + ncols] += jnp.dot(vt, w, preferred_element_type=F32)

    def body(jj, _):
        kbase = qbase - (jj + 1) * qb
        nstream = qb // stream
        pv = [None] * nstream
        for c in reversed(range(nsub)):
            kstart = pl.multiple_of(kbase + c * kb, kb)
            vt = vt_ref[:, pl.ds(kstart, kb)]
            for s in range(nstream):
                w = unit(kstart, s * stream, stream, False)
                d = jnp.dot(vt, w, preferred_element_type=F32)
                pv[s] = d if pv[s] is None else pv[s] + d
        for s in range(nstream):
            acc_ref[:, s * stream:(s + 1) * stream] += pv[s]
        return 0

    lax.fori_loop(0, i, body, 0)

    g = g_ref[...]
    out = acc_ref[...].T * (g * jax.nn.sigmoid(g))
    o_ref[...] = out.astype(BF16)


def _sb_attention(q, k, v, g, batch, seq):
    qb, kb = SB_QB, SB_KB
    nq = seq // qb
    tri = (jnp.arange(kb)[None, :] >= jnp.arange(kb)[:, None]).astype(BF16)
    qmap = lambda b, h, i: (b * nq + i, h)
    kvmap = lambda b, h, i: (b, h)
    return pl.pallas_call(
        functools.partial(_sb_kernel, seq=seq, qb=qb, kb=kb, stream=SB_STREAM),
        grid=(batch, SB_HEADS, nq),
        in_specs=[
            pl.BlockSpec((qb, SB_HEAD_DIM), qmap),
            pl.BlockSpec((seq, SB_HEAD_DIM), kvmap),
            pl.BlockSpec((seq, SB_HEAD_DIM), kvmap),
            pl.BlockSpec((qb, SB_HEAD_DIM), qmap),
            pl.BlockSpec((kb, kb), lambda b, h, i: (0, 0)),
        ],
        out_specs=pl.BlockSpec((qb, SB_HEAD_DIM), qmap),
        out_shape=jax.ShapeDtypeStruct((batch * seq, D_MODEL), BF16),
        scratch_shapes=[
            pltpu.VMEM((SB_HEAD_DIM, seq), BF16),
            pltpu.VMEM((SB_HEAD_DIM, qb), BF16),
            pltpu.VMEM((SB_HEAD_DIM, qb), F32),
            pltpu.VMEM((1, qb), F32),
        ],
        compiler_params=pltpu.CompilerParams(
            dimension_semantics=("arbitrary", "arbitrary", "arbitrary"),
            vmem_limit_bytes=VMEM_LIMIT_BYTES),
        name="stickbreak",
    )(q, k, v, g, tri)


def _ret_kernel(q_ref, k_ref, v_ref, g_ref, d_ref, qd_ref, kd_ref, gl_ref, og_ref,
                o_ref, state_ref, *, tokens, blk):
    n = pl.program_id(2)

    @pl.when(n == 0)
    def _():
        state_ref[...] = jnp.zeros_like(state_ref)

    decay = d_ref[...]
    qdec = qd_ref[...]
    kdec = kd_ref[...]
    gl = gl_ref[...]
    gain = og_ref[...]
    for c in range(tokens // blk):
        sl = slice(c * blk, (c + 1) * blk)
        q = q_ref[sl, :]
        k = k_ref[sl, :]
        v = v_ref[sl, :]
        s = lax.dot_general(q, k, (((1,), (1,)), ((), ())), preferred_element_type=F32)
        o = jnp.dot((s * decay).astype(BF16), v, preferred_element_type=F32)
        state = state_ref[...]
        o = o + qdec * jnp.dot(q, state.astype(BF16), preferred_element_type=F32)
        kdt = (k.astype(F32) * kdec).T.astype(BF16)
        state_ref[...] = state * gl + jnp.dot(kdt, v, preferred_element_type=F32)
        g = g_ref[sl, :]
        o_ref[sl, :] = (_rms(o, gain) * (g * jax.nn.sigmoid(g))).astype(BF16)


def _retention(rq, rk, rv, rg, ret_out_gain, batch, seq):
    blk = RET_L
    tokens = min(RET_T, seq)
    nt = seq // tokens
    log_gamma = jnp.log1p(-jnp.exp2(-5.0 - jnp.arange(RET_HEADS, dtype=F32)))
    t = jnp.arange(blk, dtype=F32)
    ct = jnp.arange(blk) // CHUNK
    dist = t[:, None] - t[None, :]
    lg = log_gamma[:, None, None]
    same = (ct[:, None] == ct[None, :])[None]
    earlier = (ct[None, :] < ct[:, None])[None]
    decay = jnp.where(same, jnp.exp(lg * jnp.abs(dist)[None]),
                      jnp.where(earlier, jnp.exp(lg * dist[None]), 0.0))
    ones = jnp.ones((1, 1, RET_DIM), F32)
    qdec = jnp.exp(log_gamma[:, None] * (t + 1.0)[None, :])[:, :, None] * ones
    kdec = jnp.exp(log_gamma[:, None] * (blk - 1.0 - t)[None, :])[:, :, None] * ones
    gl = jnp.exp(log_gamma * blk)[:, None, None] * ones
    gain = ret_out_gain.reshape(RET_HEADS, 1, RET_DIM)

    tmap = lambda b, h, n: (b * nt + n, h)
    hmap = lambda b, h, n: (h, 0, 0)
    tok_spec = pl.BlockSpec((tokens, RET_DIM), tmap)
    return pl.pallas_call(
        functools.partial(_ret_kernel, tokens=tokens, blk=blk),
        grid=(batch, RET_HEADS, nt),
        in_specs=[
            tok_spec, tok_spec, tok_spec, tok_spec,
            pl.BlockSpec((None, blk, blk), hmap),
            pl.BlockSpec((None, blk, RET_DIM), hmap),
            pl.BlockSpec((None, blk, RET_DIM), hmap),
            pl.BlockSpec((None, 1, RET_DIM), hmap),
            pl.BlockSpec((None, 1, RET_DIM), hmap),
        ],
        out_specs=tok_spec,
        out_shape=jax.ShapeDtypeStruct((batch * seq, D_MODEL), BF16),
        scratch_shapes=[pltpu.VMEM((RET_DIM, RET_DIM), F32)],
        compiler_params=pltpu.CompilerParams(
            dimension_semantics=("arbitrary", "arbitrary", "arbitrary"),
            vmem_limit_bytes=VMEM_LIMIT_BYTES),
        name="retention",
    )(rq, rk, rv, rg, decay, qdec, kdec, gl, gain)


def _out_kernel(oa_ref, ob_ref, msb_ref, mret_ref, bm_ref, x_ref, wsb_ref, wret_ref,
                wout_ref, o_ref):
    pa = jnp.dot(oa_ref[...], wsb_ref[...], preferred_element_type=F32)
    pb = jnp.dot(ob_ref[...], wret_ref[...], preferred_element_type=F32)
    merged = (jax.nn.sigmoid(msb_ref[...] + bm_ref[0:1, :]) * pa
              + jax.nn.sigmoid(mret_ref[...] + bm_ref[1:2, :]) * pb)
    o_ref[...] = x_ref[...] + jnp.dot(merged.astype(BF16), wout_ref[...],
                                      preferred_element_type=F32)


def _out_proj(oa, ob, msb, mret, b_merge, x2, wsb, wret, wout):
    m = x2.shape[0]
    tm = OUT_TM
    row = lambda i: (i, 0)
    const = lambda i: (0, 0)
    act = pl.BlockSpec((tm, D_MODEL), row)
    wspec = pl.BlockSpec((D_MODEL, D_MODEL), const)
    return pl.pallas_call(
        _out_kernel,
        grid=(m // tm,),
        in_specs=[act, act, act, act, pl.BlockSpec((2, D_MODEL), const), act,
                  wspec, wspec, wspec],
        out_specs=act,
        out_shape=jax.ShapeDtypeStruct((m, D_MODEL), F32),
        compiler_params=pltpu.CompilerParams(
            dimension_semantics=("arbitrary",),
            vmem_limit_bytes=VMEM_LIMIT_BYTES),
        name="outproj",
    )(oa, ob, msb, mret, b_merge, x2, wsb, wret, wout)


def _rope_tables(seq):
    d = RET_DIM
    inv_freq = ROPE_BASE ** (-jnp.arange(0, d, 2, dtype=F32) / d)
    ang = jnp.arange(seq, dtype=F32)[:, None] * inv_freq[None, :]
    return jnp.cos(ang), jnp.sin(ang)


def kernel(x, norm_gain, w_in, b_merge, sb_q_gain, sb_k_gain, ret_out_gain,
           w_branch_sb, w_branch_ret, w_out):
    batch, seq, d_model = x.shape
    depth = norm_gain.shape[0]
    assert d_model == D_MODEL and w_in.shape[-1] == N_SEGMENTS * D_MODEL
    assert seq % max(SB_QB, IN_TM, OUT_TM, RET_L) == 0 and seq % min(RET_T, seq) == 0
    cos, sin = _rope_tables(seq)
    x2 = x.reshape(batch * seq, D_MODEL)
    for layer in range(depth):
        q, k, v, sg, rq, rk, rv, rg, msb, mret = _inproj(
            x2, norm_gain[layer][None, :], w_in[layer].astype(BF16),
            sb_q_gain[layer][None, :], sb_k_gain[layer][None, :], cos, sin, seq)
        out_a = _sb_attention(q, k, v, sg, batch, seq)
        out_b = _retention(rq, rk, rv, rg, ret_out_gain[layer], batch, seq)
        x2 = _out_proj(out_a, out_b, msb, mret, b_merge[layer], x2,
                       w_branch_sb[layer].astype(BF16), w_branch_ret[layer].astype(BF16),
                       w_out[layer].astype(BF16))
    return x2.reshape(batch, seq, D_MODEL)
```

```python
import functools
import math

import jax
import jax.numpy as jnp
from jax import lax
from jax.experimental import pallas as pl
from jax.experimental.pallas import tpu as pltpu

D_MODEL = 1024
SB_HEADS = 8
SB_HEAD_DIM = D_MODEL // SB_HEADS
RET_HEADS = 4
RET_DIM = D_MODEL // RET_HEADS
CHUNK = 64
ROPE_BASE = 10000.0
EPS = 1e-6
N_SEGMENTS = 10
LOG2E = 1.4426950408889634

VMEM_LIMIT_BYTES = 56 * 1024 * 1024

IN_TM = 256
SB_QB = 1024
SB_KB = 256
SB_DONE_LOG2 = 160.0
RET_L = 256
RET_T = 1024
OUT_TM = 512

F32 = jnp.float32
BF16 = jnp.bfloat16


def _rms(x, gain):
    y = x * lax.rsqrt(jnp.mean(x * x, axis=-1, keepdims=True) + EPS)
    return y * gain


def _inproj_kernel(x_ref, ng_ref, w_ref, qg_ref, kg_ref, cos_ref, sin_ref,
                   q_ref, k_ref, v_ref, sg_ref, rq_ref, rk_ref, rv_ref, rg_ref,
                   msb_ref, mret_ref):
    hb = _rms(x_ref[...], ng_ref[...]).astype(BF16)

    def seg(s):
        return jnp.dot(hb, w_ref[:, s * D_MODEL:(s + 1) * D_MODEL],
                       preferred_element_type=F32)

    def head_norm(p, gain_ref, out_ref, post_scale=None):
        for h in range(SB_HEADS):
            sl = slice(h * SB_HEAD_DIM, (h + 1) * SB_HEAD_DIM)
            y = _rms(p[:, sl], gain_ref[...])
            if post_scale is not None:
                y = y * post_scale
            out_ref[:, sl] = y.astype(BF16)

    def rotary(p, out_ref, post_scale):
        cos = cos_ref[...]
        sin = sin_ref[...]
        half = RET_DIM // 2
        for h in range(RET_HEADS):
            t1 = p[:, h * RET_DIM:h * RET_DIM + half]
            t2 = p[:, h * RET_DIM + half:(h + 1) * RET_DIM]
            o1 = t1 * cos - t2 * sin
            o2 = t1 * sin + t2 * cos
            if post_scale is not None:
                o1 = o1 * post_scale
                o2 = o2 * post_scale
            out_ref[:, h * RET_DIM:h * RET_DIM + half] = o1.astype(BF16)
            out_ref[:, h * RET_DIM + half:(h + 1) * RET_DIM] = o2.astype(BF16)

    head_norm(seg(0), qg_ref, q_ref, (SB_HEAD_DIM ** -0.5) * LOG2E)
    head_norm(seg(1), kg_ref, k_ref)
    v_ref[...] = seg(2).astype(BF16)
    sg_ref[...] = seg(3)
    rotary(seg(4), rq_ref, None)
    rotary(seg(5), rk_ref, RET_DIM ** -0.5)
    rv_ref[...] = seg(6).astype(BF16)
    rg_ref[...] = seg(7)
    msb_ref[...] = seg(8)
    mret_ref[...] = seg(9)


def _inproj(x2, norm_gain, w_in_bf, q_gain, k_gain, cos, sin, seq):
    m = x2.shape[0]
    tm = IN_TM
    pos_blocks = seq // tm
    row = lambda i: (i, 0)
    const = lambda i: (0, 0)
    pos = lambda i: (i % pos_blocks, 0)
    bf_out = jax.ShapeDtypeStruct((m, D_MODEL), BF16)
    f32_out = jax.ShapeDtypeStruct((m, D_MODEL), F32)
    out_spec = pl.BlockSpec((tm, D_MODEL), row)
    return pl.pallas_call(
        _inproj_kernel,
        grid=(m // tm,),
        in_specs=[
            pl.BlockSpec((tm, D_MODEL), row),
            pl.BlockSpec((1, D_MODEL), const),
            pl.BlockSpec((D_MODEL, N_SEGMENTS * D_MODEL), const),
            pl.BlockSpec((1, SB_HEAD_DIM), const),
            pl.BlockSpec((1, SB_HEAD_DIM), const),
            pl.BlockSpec((tm, RET_DIM // 2), pos),
            pl.BlockSpec((tm, RET_DIM // 2), pos),
        ],
        out_specs=[out_spec] * 10,
        out_shape=[bf_out, bf_out, bf_out, f32_out, bf_out, bf_out, bf_out,
                   f32_out, f32_out, f32_out],
        compiler_params=pltpu.CompilerParams(
            dimension_semantics=("arbitrary",),
            vmem_limit_bytes=VMEM_LIMIT_BYTES),
        name="inproj",
    )(x2, norm_gain, w_in_bf, q_gain, k_gain, cos, sin)


def _sb_kernel(q_ref, k_ref, v_ref, g_ref, tri_ref, o_ref, vt_ref, qt_ref, acc_ref,
               carry_ref, *, seq, qb, kb):
    i = pl.program_id(2)
    nchunk = qb // kb

    @pl.when(i == 0)
    def _():
        step = 512
        for c in range(seq // step):
            vt_ref[:, c * step:(c + 1) * step] = (
                v_ref[c * step:(c + 1) * step, :].astype(F32).T.astype(BF16))

    qt_ref[...] = q_ref[...].astype(F32).T.astype(BF16)
    tri = tri_ref[...]
    acc_ref[...] = jnp.zeros_like(acc_ref)
    carry_ref[...] = jnp.zeros_like(carry_ref)

    def unit(c, tile, masked):
        cols = slice(c * kb, (c + 1) * kb)
        kstart = pl.multiple_of(tile * kb, kb)
        kt = k_ref[pl.ds(kstart, kb), :]
        z2 = jnp.dot(kt, qt_ref[:, cols], preferred_element_type=F32)
        sp2 = jnp.maximum(z2, 0.0) + jnp.log(1.0 + jnp.exp2(-jnp.abs(z2))) * LOG2E
        if masked:
            row = lax.broadcasted_iota(jnp.int32, z2.shape, 0)
            col = lax.broadcasted_iota(jnp.int32, z2.shape, 1)
            mask = row < col
            sp2 = jnp.where(mask, sp2, 0.0)
        cs = jnp.dot(tri, sp2.astype(BF16), preferred_element_type=F32)
        carry = carry_ref[:, cols]
        w = jnp.exp2(z2 - cs - carry)
        if masked:
            w = jnp.where(mask, w, 0.0)
        carry_ref[:, cols] = carry + cs[0:1, :]
        vt = vt_ref[:, pl.ds(kstart, kb)]
        acc_ref[:, cols] += jnp.dot(vt, w.astype(BF16), preferred_element_type=F32)

    def unfinished(c_from=0):
        return jnp.min(carry_ref[:, c_from * kb:]) < SB_DONE_LOG2

    first = i * nchunk

    def first_two(chunk0_has_prev):
        for c in range(nchunk):
            unit(c, first + c, True)
        for c in range(0 if chunk0_has_prev else 1, nchunk):
            unit(c, first + c - 1, False)

    pl.when(i > 0)(functools.partial(first_two, True))
    pl.when(i == 0)(functools.partial(first_two, False))

    def more(state):
        d, go = state
        return jnp.logical_and(d <= first, go)

    def sweep(state):
        d, _ = state
        for c in range(nchunk):
            unit(c, first + c - d, False)
        return d + 1, unfinished()

    _, go = lax.while_loop(more, sweep, (jnp.int32(2), jnp.logical_and(i > 0, unfinished())))

    go = jnp.logical_or(go, i == 0)
    for e in range(1, nchunk):
        @pl.when(jnp.logical_and(go, i > 0) if e == 1 else go)
        def _():
            for c in range(e, nchunk):
                unit(c, c - e, False)
        if e + 1 < nchunk:
            go = jnp.logical_and(go, unfinished(e + 1))

    g = g_ref[...]
    out = acc_ref[...].T * (g * jax.nn.sigmoid(g))
    o_ref[...] = out.astype(BF16)


def _sb_attention(q, k, v, g, batch, seq):
    qb, kb = SB_QB, SB_KB
    nq = seq // qb
    tri = (jnp.arange(kb)[None, :] >= jnp.arange(kb)[:, None]).astype(BF16)
    qmap = lambda b, h, i: (b * nq + i, h)
    kvmap = lambda b, h, i: (b, h)
    return pl.pallas_call(
        functools.partial(_sb_kernel, seq=seq, qb=qb, kb=kb),
        grid=(batch, SB_HEADS, nq),
        in_specs=[
            pl.BlockSpec((qb, SB_HEAD_DIM), qmap),
            pl.BlockSpec((seq, SB_HEAD_DIM), kvmap),
            pl.BlockSpec((seq, SB_HEAD_DIM), kvmap),
            pl.BlockSpec((qb, SB_HEAD_DIM), qmap),
            pl.BlockSpec((kb, kb), lambda b, h, i: (0, 0)),
        ],
        out_specs=pl.BlockSpec((qb, SB_HEAD_DIM), qmap),
        out_shape=jax.ShapeDtypeStruct((batch * seq, D_MODEL), BF16),
        scratch_shapes=[
            pltpu.VMEM((SB_HEAD_DIM, seq), BF16),
            pltpu.VMEM((SB_HEAD_DIM, qb), BF16),
            pltpu.VMEM((SB_HEAD_DIM, qb), F32),
            pltpu.VMEM((1, qb), F32),
        ],
        compiler_params=pltpu.CompilerParams(
            dimension_semantics=("arbitrary", "arbitrary", "arbitrary"),
            vmem_limit_bytes=VMEM_LIMIT_BYTES),
        name="stickbreak",
    )(q, k, v, g, tri)


def _ret_kernel(q_ref, k_ref, v_ref, g_ref, d_ref, qd_ref, kd_ref, gl_ref, og_ref,
                o_ref, state_ref, *, tokens, blk):
    n = pl.program_id(2)

    @pl.when(n == 0)
    def _():
        state_ref[...] = jnp.zeros_like(state_ref)

    decay = d_ref[...]
    qdec = qd_ref[...]
    kdec = kd_ref[...]
    gl = gl_ref[...]
    gain = og_ref[...]
    for c in range(tokens // blk):
        sl = slice(c * blk, (c + 1) * blk)
        q = q_ref[sl, :]
        k = k_ref[sl, :]
        v = v_ref[sl, :]
        s = lax.dot_general(q, k, (((1,), (1,)), ((), ())), preferred_element_type=F32)
        o = jnp.dot((s * decay).astype(BF16), v, preferred_element_type=F32)
        state = state_ref[...]
        o = o + qdec * jnp.dot(q, state.astype(BF16), preferred_element_type=F32)
        kdt = (k.astype(F32) * kdec).T.astype(BF16)
        state_ref[...] = state * gl + jnp.dot(kdt, v, preferred_element_type=F32)
        g = g_ref[sl, :]
        o_ref[sl, :] = (_rms(o, gain) * (g * jax.nn.sigmoid(g))).astype(BF16)


def _retention(rq, rk, rv, rg, ret_out_gain, batch, seq):
    blk = RET_L
    tokens = min(RET_T, seq)
    nt = seq // tokens
    log_gamma = jnp.log1p(-jnp.exp2(-5.0 - jnp.arange(RET_HEADS, dtype=F32)))
    t = jnp.arange(blk, dtype=F32)
    ct = jnp.arange(blk) // CHUNK
    dist = t[:, None] - t[None, :]
    lg = log_gamma[:, None, None]
    same = (ct[:, None] == ct[None, :])[None]
    earlier = (ct[None, :] < ct[:, None])[None]
    decay = jnp.where(same, jnp.exp(lg * jnp.abs(dist)[None]),
                      jnp.where(earlier, jnp.exp(lg * dist[None]), 0.0))
    ones = jnp.ones((1, 1, RET_DIM), F32)
    qdec = jnp.exp(log_gamma[:, None] * (t + 1.0)[None, :])[:, :, None] * ones
    kdec = jnp.exp(log_gamma[:, None] * (blk - 1.0 - t)[None, :])[:, :, None] * ones
    gl = jnp.exp(log_gamma * blk)[:, None, None] * ones
    gain = ret_out_gain.reshape(RET_HEADS, 1, RET_DIM)

    tmap = lambda b, h, n: (b * nt + n, h)
    hmap = lambda b, h, n: (h, 0, 0)
    tok_spec = pl.BlockSpec((tokens, RET_DIM), tmap)
    return pl.pallas_call(
        functools.partial(_ret_kernel, tokens=tokens, blk=blk),
        grid=(batch, RET_HEADS, nt),
        in_specs=[
            tok_spec, tok_spec, tok_spec, tok_spec,
            pl.BlockSpec((None, blk, blk), hmap),
            pl.BlockSpec((None, blk, RET_DIM), hmap),
            pl.BlockSpec((None, blk, RET_DIM), hmap),
            pl.BlockSpec((None, 1, RET_DIM), hmap),
            pl.BlockSpec((None, 1, RET_DIM), hmap),
        ],
        out_specs=tok_spec,
        out_shape=jax.ShapeDtypeStruct((batch * seq, D_MODEL), BF16),
        scratch_shapes=[pltpu.VMEM((RET_DIM, RET_DIM), F32)],
        compiler_params=pltpu.CompilerParams(
            dimension_semantics=("arbitrary", "arbitrary", "arbitrary"),
            vmem_limit_bytes=VMEM_LIMIT_BYTES),
        name="retention",
    )(rq, rk, rv, rg, decay, qdec, kdec, gl, gain)


def _out_kernel(oa_ref, ob_ref, msb_ref, mret_ref, bm_ref, x_ref, wsb_ref, wret_ref,
                wout_ref, o_ref):
    pa = jnp.dot(oa_ref[...], wsb_ref[...], preferred_element_type=F32)
    pb = jnp.dot(ob_ref[...], wret_ref[...], preferred_element_type=F32)
    merged = (jax.nn.sigmoid(msb_ref[...] + bm_ref[0:1, :]) * pa
              + jax.nn.sigmoid(mret_ref[...] + bm_ref[1:2, :]) * pb)
    o_ref[...] = x_ref[...] + jnp.dot(merged.astype(BF16), wout_ref[...],
                                      preferred_element_type=F32)


def _out_proj(oa, ob, msb, mret, b_merge, x2, wsb, wret, wout):
    m = x2.shape[0]
    tm = OUT_TM
    row = lambda i: (i, 0)
    const = lambda i: (0, 0)
    act = pl.BlockSpec((tm, D_MODEL), row)
    wspec = pl.BlockSpec((D_MODEL, D_MODEL), const)
    return pl.pallas_call(
        _out_kernel,
        grid=(m // tm,),
        in_specs=[act, act, act, act, pl.BlockSpec((2, D_MODEL), const), act,
                  wspec, wspec, wspec],
        out_specs=act,
        out_shape=jax.ShapeDtypeStruct((m, D_MODEL), F32),
        compiler_params=pltpu.CompilerParams(
            dimension_semantics=("arbitrary",),
            vmem_limit_bytes=VMEM_LIMIT_BYTES),
        name="outproj",
    )(oa, ob, msb, mret, b_merge, x2, wsb, wret, wout)


def _rope_tables(seq):
    d = RET_DIM
    inv_freq = ROPE_BASE ** (-jnp.arange(0, d, 2, dtype=F32) / d)
    ang = jnp.arange(seq, dtype=F32)[:, None] * inv_freq[None, :]
    return jnp.cos(ang), jnp.sin(ang)


def kernel(x, norm_gain, w_in, b_merge, sb_q_gain, sb_k_gain, ret_out_gain,
           w_branch_sb, w_branch_ret, w_out):
    batch, seq, d_model = x.shape
    depth = norm_gain.shape[0]
    assert d_model == D_MODEL and w_in.shape[-1] == N_SEGMENTS * D_MODEL
    assert seq % max(SB_QB, IN_TM, OUT_TM, RET_L) == 0 and seq % min(RET_T, seq) == 0
    cos, sin = _rope_tables(seq)
    x2 = x.reshape(batch * seq, D_MODEL)
    for layer in range(depth):
        q, k, v, sg, rq, rk, rv, rg, msb, mret = _inproj(
            x2, norm_gain[layer][None, :], w_in[layer].astype(BF16),
            sb_q_gain[layer][None, :], sb_k_gain[layer][None, :], cos, sin, seq)
        out_a = _sb_attention(q, k, v, sg, batch, seq)
        out_b = _retention(rq, rk, rv, rg, ret_out_gain[layer], batch, seq)
        x2 = _out_proj(out_a, out_b, msb, mret, b_merge[layer], x2,
                       w_branch_sb[layer].astype(BF16), w_branch_ret[layer].astype(BF16),
                       w_out[layer].astype(BF16))
    return x2.reshape(batch, seq, D_MODEL)
```

```python
import functools
import math

import jax
import jax.numpy as jnp
from jax import lax
from jax.experimental import pallas as pl
from jax.experimental.pallas import tpu as pltpu

D_MODEL = 1024
SB_HEADS = 8
SB_HEAD_DIM = D_MODEL // SB_HEADS
RET_HEADS = 4
RET_DIM = D_MODEL // RET_HEADS
CHUNK = 64
ROPE_BASE = 10000.0
EPS = 1e-6
N_SEGMENTS = 10
LOG2E = 1.4426950408889634

VMEM_LIMIT_BYTES = 56 * 1024 * 1024

IN_TM = 256
SB_QB = 1024
SB_KB = 256
SB_DONE_LOG2 = 160.0
RET_L = 256
RET_T = 1024
OUT_TM = 512

F32 = jnp.float32
BF16 = jnp.bfloat16


def _rms(x, gain):
    y = x * lax.rsqrt(jnp.mean(x * x, axis=-1, keepdims=True) + EPS)
    return y * gain


def _inproj_kernel(x_ref, ng_ref, w_ref, qg_ref, kg_ref, cos_ref, sin_ref,
                   q_ref, k_ref, v_ref, sg_ref, rq_ref, rk_ref, rv_ref, rg_ref,
                   msb_ref, mret_ref):
    hb = _rms(x_ref[...], ng_ref[...]).astype(BF16)

    def seg(s):
        return jnp.dot(hb, w_ref[:, s * D_MODEL:(s + 1) * D_MODEL],
                       preferred_element_type=F32)

    def head_norm(p, gain_ref, out_ref, post_scale=None):
        for h in range(SB_HEADS):
            sl = slice(h * SB_HEAD_DIM, (h + 1) * SB_HEAD_DIM)
            y = _rms(p[:, sl], gain_ref[...])
            if post_scale is not None:
                y = y * post_scale
            out_ref[:, sl] = y.astype(BF16)

    def rotary(p, out_ref, post_scale):
        cos = cos_ref[...]
        sin = sin_ref[...]
        half = RET_DIM // 2
        for h in range(RET_HEADS):
            t1 = p[:, h * RET_DIM:h * RET_DIM + half]
            t2 = p[:, h * RET_DIM + half:(h + 1) * RET_DIM]
            o1 = t1 * cos - t2 * sin
            o2 = t1 * sin + t2 * cos
            if post_scale is not None:
                o1 = o1 * post_scale
                o2 = o2 * post_scale
            out_ref[:, h * RET_DIM:h * RET_DIM + half] = o1.astype(BF16)
            out_ref[:, h * RET_DIM + half:(h + 1) * RET_DIM] = o2.astype(BF16)

    head_norm(seg(0), qg_ref, q_ref, (SB_HEAD_DIM ** -0.5) * LOG2E)
    head_norm(seg(1), kg_ref, k_ref)
    v_ref[...] = seg(2).astype(BF16)
    sg_ref[...] = seg(3)
    rotary(seg(4), rq_ref, None)
    rotary(seg(5), rk_ref, RET_DIM ** -0.5)
    rv_ref[...] = seg(6).astype(BF16)
    rg_ref[...] = seg(7)
    msb_ref[...] = seg(8)
    mret_ref[...] = seg(9)


def _inproj(x2, norm_gain, w_in_bf, q_gain, k_gain, cos, sin, seq):
    m = x2.shape[0]
    tm = IN_TM
    pos_blocks = seq // tm
    row = lambda i: (i, 0)
    const = lambda i: (0, 0)
    pos = lambda i: (i % pos_blocks, 0)
    bf_out = jax.ShapeDtypeStruct((m, D_MODEL), BF16)
    f32_out = jax.ShapeDtypeStruct((m, D_MODEL), F32)
    out_spec = pl.BlockSpec((tm, D_MODEL), row)
    return pl.pallas_call(
        _inproj_kernel,
        grid=(m // tm,),
        in_specs=[
            pl.BlockSpec((tm, D_MODEL), row),
            pl.BlockSpec((1, D_MODEL), const),
            pl.BlockSpec((D_MODEL, N_SEGMENTS * D_MODEL), const),
            pl.BlockSpec((1, SB_HEAD_DIM), const),
            pl.BlockSpec((1, SB_HEAD_DIM), const),
            pl.BlockSpec((tm, RET_DIM // 2), pos),
            pl.BlockSpec((tm, RET_DIM // 2), pos),
        ],
        out_specs=[out_spec] * 10,
        out_shape=[bf_out, bf_out, bf_out, f32_out, bf_out, bf_out, bf_out,
                   f32_out, f32_out, f32_out],
        compiler_params=pltpu.CompilerParams(
            dimension_semantics=("arbitrary",),
            vmem_limit_bytes=VMEM_LIMIT_BYTES),
        name="inproj",
    )(x2, norm_gain, w_in_bf, q_gain, k_gain, cos, sin)


def _sb_kernel(q_ref, k_ref, v_ref, g_ref, tri_ref, o_ref, vt_ref, qt_ref, acc_ref,
               carry_ref, z_ref, sp_ref, *, seq, qb, kb):
    i = pl.program_id(2)
    nchunk = qb // kb

    @pl.when(i == 0)
    def _():
        step = 512
        for c in range(seq // step):
            vt_ref[:, c * step:(c + 1) * step] = (
                v_ref[c * step:(c + 1) * step, :].astype(F32).T.astype(BF16))

    qt_ref[...] = q_ref[...].astype(F32).T.astype(BF16)
    tri = tri_ref[...]

    def causal_mask():
        row = lax.broadcasted_iota(jnp.int32, (kb, kb), 0)
        col = lax.broadcasted_iota(jnp.int32, (kb, kb), 1)
        return row < col

    def units(todo):
        assert len(todo) <= z_ref.shape[0]
        kstarts = [pl.multiple_of(tile * kb, kb) for _, tile, _ in todo]
        for u, (c, _, _) in enumerate(todo):
            kt = k_ref[pl.ds(kstarts[u], kb), :]
            z_ref[u] = jnp.dot(kt, qt_ref[:, c * kb:(c + 1) * kb],
                               preferred_element_type=F32)
        for u, (_, _, masked) in enumerate(todo):
            z2 = z_ref[u]
            sp2 = jnp.maximum(z2, 0.0) + jnp.log(1.0 + jnp.exp2(-jnp.abs(z2))) * LOG2E
            if masked:
                sp2 = jnp.where(causal_mask(), sp2, 0.0)
            sp_ref[u] = sp2.astype(BF16)
        for u, (c, _, masked) in enumerate(todo):
            cols = slice(c * kb, (c + 1) * kb)
            cs = jnp.dot(tri, sp_ref[u], preferred_element_type=F32)
            carry = carry_ref[:, cols]
            w = jnp.exp2(z_ref[u] - cs - carry)
            if masked:
                w = jnp.where(causal_mask(), w, 0.0)
            carry_ref[:, cols] = carry + cs[0:1, :]
            vt = vt_ref[:, pl.ds(kstarts[u], kb)]
            acc_ref[:, cols] += jnp.dot(vt, w.astype(BF16), preferred_element_type=F32)

    def unfinished(c_from=0):
        return jnp.min(carry_ref[:, c_from * kb:]) < SB_DONE_LOG2

    first = i * nchunk

    def first_two(chunk0_has_prev):
        def kstart_of(c, d):
            return pl.multiple_of((first + c - d) * kb, kb)

        todo = [(c, d) for d in (0, 1) for c in range(nchunk)
                if d == 0 or c > 0 or chunk0_has_prev]
        slot = {cd: u for u, cd in enumerate(todo)}
        for (c, d), u in slot.items():
            kt = k_ref[pl.ds(kstart_of(c, d), kb), :]
            z_ref[u] = jnp.dot(kt, qt_ref[:, c * kb:(c + 1) * kb],
                               preferred_element_type=F32)
        for (c, d), u in slot.items():
            z2 = z_ref[u]
            sp2 = jnp.maximum(z2, 0.0) + jnp.log(1.0 + jnp.exp2(-jnp.abs(z2))) * LOG2E
            if d == 0:
                sp2 = jnp.where(causal_mask(), sp2, 0.0)
            sp_ref[u] = sp2.astype(BF16)
        total = {}
        for cd, u in slot.items():
            cs = jnp.dot(tri, sp_ref[u], preferred_element_type=F32)
            z_ref[u] = z_ref[u] - cs
            total[cd] = cs[0:1, :]
        for c in range(nchunk):
            cols = slice(c * kb, (c + 1) * kb)
            w = jnp.where(causal_mask(), jnp.exp2(z_ref[slot[(c, 0)]]), 0.0)
            carry = total[(c, 0)]
            pv = jnp.dot(vt_ref[:, pl.ds(kstart_of(c, 0), kb)], w.astype(BF16),
                         preferred_element_type=F32)
            if (c, 1) in slot:
                w = jnp.exp2(z_ref[slot[(c, 1)]] - carry)
                carry = carry + total[(c, 1)]
                pv = pv + jnp.dot(vt_ref[:, pl.ds(kstart_of(c, 1), kb)], w.astype(BF16),
                                  preferred_element_type=F32)
            acc_ref[:, cols] = pv
            carry_ref[:, cols] = carry

    pl.when(i > 0)(functools.partial(first_two, True))
    pl.when(i == 0)(functools.partial(first_two, False))

    def more(state):
        d, go = state
        return jnp.logical_and(d <= first, go)

    def sweep(state):
        d, _ = state
        units([(c, first + c - d, False) for c in range(nchunk)])
        return d + 1, unfinished()

    _, go = lax.while_loop(more, sweep, (jnp.int32(2), jnp.logical_and(i > 0, unfinished())))

    go = jnp.logical_or(go, i == 0)
    for e in range(1, nchunk):
        @pl.when(jnp.logical_and(go, i > 0) if e == 1 else go)
        def _():
            units([(c, c - e, False) for c in range(e, nchunk)])
        if e + 1 < nchunk:
            go = jnp.logical_and(go, unfinished(e + 1))

    g = g_ref[...]
    out = acc_ref[...].T * (g * jax.nn.sigmoid(g))
    o_ref[...] = out.astype(BF16)


def _sb_attention(q, k, v, g, batch, seq):
    qb, kb = SB_QB, SB_KB
    nq = seq // qb
    tri = (jnp.arange(kb)[None, :] >= jnp.arange(kb)[:, None]).astype(BF16)
    qmap = lambda b, h, i: (b * nq + i, h)
    kvmap = lambda b, h, i: (b, h)
    return pl.pallas_call(
        functools.partial(_sb_kernel, seq=seq, qb=qb, kb=kb),
        grid=(batch, SB_HEADS, nq),
        in_specs=[
            pl.BlockSpec((qb, SB_HEAD_DIM), qmap),
            pl.BlockSpec((seq, SB_HEAD_DIM), kvmap),
            pl.BlockSpec((seq, SB_HEAD_DIM), kvmap),
            pl.BlockSpec((qb, SB_HEAD_DIM), qmap),
            pl.BlockSpec((kb, kb), lambda b, h, i: (0, 0)),
        ],
        out_specs=pl.BlockSpec((qb, SB_HEAD_DIM), qmap),
        out_shape=jax.ShapeDtypeStruct((batch * seq, D_MODEL), BF16),
        scratch_shapes=[
            pltpu.VMEM((SB_HEAD_DIM, seq), BF16),
            pltpu.VMEM((SB_HEAD_DIM, qb), BF16),
            pltpu.VMEM((SB_HEAD_DIM, qb), F32),
            pltpu.VMEM((1, qb), F32),
            pltpu.VMEM((2 * (qb // kb), kb, kb), F32),
            pltpu.VMEM((2 * (qb // kb), kb, kb), BF16),
        ],
        compiler_params=pltpu.CompilerParams(
            dimension_semantics=("arbitrary", "arbitrary", "arbitrary"),
            vmem_limit_bytes=VMEM_LIMIT_BYTES),
        name="stickbreak",
    )(q, k, v, g, tri)


def _ret_kernel(q_ref, k_ref, v_ref, g_ref, d_ref, qd_ref, kd_ref, gl_ref, og_ref,
                o_ref, state_ref, *, tokens, blk):
    n = pl.program_id(2)

    @pl.when(n == 0)
    def _():
        state_ref[...] = jnp.zeros_like(state_ref)

    decay = d_ref[...]
    qdec = qd_ref[...]
    kdec = kd_ref[...]
    gl = gl_ref[...]
    gain = og_ref[...]
    for c in range(tokens // blk):
        sl = slice(c * blk, (c + 1) * blk)
        q = q_ref[sl, :]
        k = k_ref[sl, :]
        v = v_ref[sl, :]
        s = lax.dot_general(q, k, (((1,), (1,)), ((), ())), preferred_element_type=F32)
        o = jnp.dot((s * decay).astype(BF16), v, preferred_element_type=F32)
        state = state_ref[...]
        o = o + qdec * jnp.dot(q, state.astype(BF16), preferred_element_type=F32)
        kdt = (k.astype(F32) * kdec).T.astype(BF16)
        state_ref[...] = state * gl + jnp.dot(kdt, v, preferred_element_type=F32)
        g = g_ref[sl, :]
        o_ref[sl, :] = (_rms(o, gain) * (g * jax.nn.sigmoid(g))).astype(BF16)


def _retention(rq, rk, rv, rg, ret_out_gain, batch, seq):
    blk = RET_L
    tokens = min(RET_T, seq)
    nt = seq // tokens
    log_gamma = jnp.log1p(-jnp.exp2(-5.0 - jnp.arange(RET_HEADS, dtype=F32)))
    t = jnp.arange(blk, dtype=F32)
    ct = jnp.arange(blk) // CHUNK
    dist = t[:, None] - t[None, :]
    lg = log_gamma[:, None, None]
    same = (ct[:, None] == ct[None, :])[None]
    earlier = (ct[None, :] < ct[:, None])[None]
    decay = jnp.where(same, jnp.exp(lg * jnp.abs(dist)[None]),
                      jnp.where(earlier, jnp.exp(lg * dist[None]), 0.0))
    ones = jnp.ones((1, 1, RET_DIM), F32)
    qdec = jnp.exp(log_gamma[:, None] * (t + 1.0)[None, :])[:, :, None] * ones
    kdec = jnp.exp(log_gamma[:, None] * (blk - 1.0 - t)[None, :])[:, :, None] * ones
    gl = jnp.exp(log_gamma * blk)[:, None, None] * ones
    gain = ret_out_gain.reshape(RET_HEADS, 1, RET_DIM)

    tmap = lambda b, h, n: (b * nt + n, h)
    hmap = lambda b, h, n: (h, 0, 0)
    tok_spec = pl.BlockSpec((tokens, RET_DIM), tmap)
    return pl.pallas_call(
        functools.partial(_ret_kernel, tokens=tokens, blk=blk),
        grid=(batch, RET_HEADS, nt),
        in_specs=[
            tok_spec, tok_spec, tok_spec, tok_spec,
            pl.BlockSpec((None, blk, blk), hmap),
            pl.BlockSpec((None, blk, RET_DIM), hmap),
            pl.BlockSpec((None, blk, RET_DIM), hmap),
            pl.BlockSpec((None, 1, RET_DIM), hmap),
            pl.BlockSpec((None, 1, RET_DIM), hmap),
        ],
        out_specs=tok_spec,
        out_shape=jax.ShapeDtypeStruct((batch * seq, D_MODEL), BF16),
        scratch_shapes=[pltpu.VMEM((RET_DIM, RET_DIM), F32)],
        compiler_params=pltpu.CompilerParams(
            dimension_semantics=("arbitrary", "arbitrary", "arbitrary"),
            vmem_limit_bytes=VMEM_LIMIT_BYTES),
        name="retention",
    )(rq, rk, rv, rg, decay, qdec, kdec, gl, gain)


def _out_kernel(oa_ref, ob_ref, msb_ref, mret_ref, bm_ref, x_ref, wsb_ref, wret_ref,
                wout_ref, o_ref):
    pa = jnp.dot(oa_ref[...], wsb_ref[...], preferred_element_type=F32)
    pb = jnp.dot(ob_ref[...], wret_ref[...], preferred_element_type=F32)
    merged = (jax.nn.sigmoid(msb_ref[...] + bm_ref[0:1, :]) * pa
              + jax.nn.sigmoid(mret_ref[...] + bm_ref[1:2, :]) * pb)
    o_ref[...] = x_ref[...] + jnp.dot(merged.astype(BF16), wout_ref[...],
                                      preferred_element_type=F32)


def _out_proj(oa, ob, msb, mret, b_merge, x2, wsb, wret, wout):
    m = x2.shape[0]
    tm = OUT_TM
    row = lambda i: (i, 0)
    const = lambda i: (0, 0)
    act = pl.BlockSpec((tm, D_MODEL), row)
    wspec = pl.BlockSpec((D_MODEL, D_MODEL), const)
    return pl.pallas_call(
        _out_kernel,
        grid=(m // tm,),
        in_specs=[act, act, act, act, pl.BlockSpec((2, D_MODEL), const), act,
                  wspec, wspec, wspec],
        out_specs=act,
        out_shape=jax.ShapeDtypeStruct((m, D_MODEL), F32),
        compiler_params=pltpu.CompilerParams(
            dimension_semantics=("arbitrary",),
            vmem_limit_bytes=VMEM_LIMIT_BYTES),
        name="outproj",
    )(oa, ob, msb, mret, b_merge, x2, wsb, wret, wout)


def _rope_tables(seq):
    d = RET_DIM
    inv_freq = ROPE_BASE ** (-jnp.arange(0, d, 2, dtype=F32) / d)
    ang = jnp.arange(seq, dtype=F32)[:, None] * inv_freq[None, :]
    return jnp.cos(ang), jnp.sin(ang)


def kernel(x, norm_gain, w_in, b_merge, sb_q_gain, sb_k_gain, ret_out_gain,
           w_branch_sb, w_branch_ret, w_out):
    batch, seq, d_model = x.shape
    depth = norm_gain.shape[0]
    assert d_model == D_MODEL and w_in.shape[-1] == N_SEGMENTS * D_MODEL
    assert seq % max(SB_QB, IN_TM, OUT_TM, RET_L) == 0 and seq % min(RET_T, seq) == 0
    cos, sin = _rope_tables(seq)
    x2 = x.reshape(batch * seq, D_MODEL)
    for layer in range(depth):
        q, k, v, sg, rq, rk, rv, rg, msb, mret = _inproj(
            x2, norm_gain[layer][None, :], w_in[layer].astype(BF16),
            sb_q_gain[layer][None, :], sb_k_gain[layer][None, :], cos, sin, seq)
        out_a = _sb_attention(q, k, v, sg, batch, seq)
        out_b = _retention(rq, rk, rv, rg, ret_out_gain[layer], batch, seq)
        x2 = _out_proj(out_a, out_b, msb, mret, b_merge[layer], x2,
                       w_branch_sb[layer].astype(BF16), w_branch_ret[layer].astype(BF16),
                       w_out[layer].astype(BF16))
    return x2.reshape(batch, seq, D_MODEL)
```

```python
import functools
import math

import jax
import jax.numpy as jnp
from jax import lax
from jax.experimental import pallas as pl
from jax.experimental.pallas import tpu as pltpu

D_MODEL = 1024
SB_HEADS = 8
SB_HEAD_DIM = D_MODEL // SB_HEADS
RET_HEADS = 4
RET_DIM = D_MODEL // RET_HEADS
CHUNK = 64
ROPE_BASE = 10000.0
EPS = 1e-6
N_SEGMENTS = 10
LOG2E = 1.4426950408889634

VMEM_LIMIT_BYTES = 56 * 1024 * 1024

IN_TM = 256
SB_QB = 1024
SB_KB = 256
SB_DONE_LOG2 = 160.0
RET_L = 256
RET_T = 2048
OUT_TM = 512

F32 = jnp.float32
BF16 = jnp.bfloat16


def _rms(x, gain):
    y = x * lax.rsqrt(jnp.mean(x * x, axis=-1, keepdims=True) + EPS)
    return y * gain


def _silu(g):
    return g * jax.nn.sigmoid(g)


def _inproj_kernel(x_ref, ng_ref, w_ref, qg_ref, kg_ref, cos_ref, sin_ref, bm_ref,
                   qt_ref, k_ref, vt_ref, sg_ref, rq_ref, rk_ref, rv_ref, rg_ref,
                   msb_ref, mret_ref):
    hb = _rms(x_ref[...], ng_ref[...]).astype(BF16)

    def seg(s):
        return jnp.dot(hb, w_ref[:, s * D_MODEL:(s + 1) * D_MODEL],
                       preferred_element_type=F32)

    def heads(p, out_ref, gain_ref=None, post_scale=None, transpose=False):
        for h in range(SB_HEADS):
            sl = slice(h * SB_HEAD_DIM, (h + 1) * SB_HEAD_DIM)
            y = p[:, sl]
            if gain_ref is not None:
                y = _rms(y, gain_ref[...])
            if post_scale is not None:
                y = y * post_scale
            if transpose:
                out_ref[sl, :] = y.T.astype(BF16)
            else:
                out_ref[:, sl] = y.astype(BF16)

    def rotary(p, out_ref, post_scale):
        cos = cos_ref[...]
        sin = sin_ref[...]
        half = RET_DIM // 2
        for h in range(RET_HEADS):
            t1 = p[:, h * RET_DIM:h * RET_DIM + half]
            t2 = p[:, h * RET_DIM + half:(h + 1) * RET_DIM]
            o1 = t1 * cos - t2 * sin
            o2 = t1 * sin + t2 * cos
            if post_scale is not None:
                o1 = o1 * post_scale
                o2 = o2 * post_scale
            out_ref[:, h * RET_DIM:h * RET_DIM + half] = o1.astype(BF16)
            out_ref[:, h * RET_DIM + half:(h + 1) * RET_DIM] = o2.astype(BF16)

    heads(seg(0), qt_ref, qg_ref, (SB_HEAD_DIM ** -0.5) * LOG2E, transpose=True)
    heads(seg(1), k_ref, kg_ref)
    heads(seg(2), vt_ref, transpose=True)
    sg_ref[...] = _silu(seg(3))
    rotary(seg(4), rq_ref, None)
    rotary(seg(5), rk_ref, RET_DIM ** -0.5)
    rv_ref[...] = seg(6).astype(BF16)
    rg_ref[...] = _silu(seg(7))
    msb_ref[...] = jax.nn.sigmoid(seg(8) + bm_ref[0:1, :])
    mret_ref[...] = jax.nn.sigmoid(seg(9) + bm_ref[1:2, :])


def _inproj(x2, norm_gain, w_in_bf, q_gain, k_gain, cos, sin, b_merge, seq):
    m = x2.shape[0]
    tm = IN_TM
    pos_blocks = seq // tm
    row = lambda i: (i, 0)
    const = lambda i: (0, 0)
    pos = lambda i: (i % pos_blocks, 0)
    bf_out = jax.ShapeDtypeStruct((m, D_MODEL), BF16)
    f32_out = jax.ShapeDtypeStruct((m, D_MODEL), F32)
    out_spec = pl.BlockSpec((tm, D_MODEL), row)
    t_out = jax.ShapeDtypeStruct((m // seq, D_MODEL, seq), BF16)
    t_spec = pl.BlockSpec((None, D_MODEL, tm), lambda i: (i // pos_blocks, 0, i % pos_blocks))
    return pl.pallas_call(
        _inproj_kernel,
        grid=(m // tm,),
        in_specs=[
            pl.BlockSpec((tm, D_MODEL), row),
            pl.BlockSpec((1, D_MODEL), const),
            pl.BlockSpec((D_MODEL, N_SEGMENTS * D_MODEL), const),
            pl.BlockSpec((1, SB_HEAD_DIM), const),
            pl.BlockSpec((1, SB_HEAD_DIM), const),
            pl.BlockSpec((tm, RET_DIM // 2), pos),
            pl.BlockSpec((tm, RET_DIM // 2), pos),
            pl.BlockSpec((2, D_MODEL), const),
        ],
        out_specs=[t_spec, out_spec, t_spec] + [out_spec] * 7,
        out_shape=[t_out, bf_out, t_out, f32_out, bf_out, bf_out, bf_out,
                   f32_out, f32_out, f32_out],
        compiler_params=pltpu.CompilerParams(
            dimension_semantics=("arbitrary",),
            vmem_limit_bytes=VMEM_LIMIT_BYTES),
        name="inproj",
    )(x2, norm_gain, w_in_bf, q_gain, k_gain, cos, sin, b_merge)


def _sb_kernel(qt_ref, k_ref, vt_ref, g_ref, tri_ref, o_ref, acc_ref, carry_ref, z_ref,
               sp_ref, *, qb, kb):
    i = pl.program_id(2)
    nchunk = qb // kb
    tri = tri_ref[...]

    def causal_mask():
        row = lax.broadcasted_iota(jnp.int32, (kb, kb), 0)
        col = lax.broadcasted_iota(jnp.int32, (kb, kb), 1)
        return row < col

    def units(todo):
        assert len(todo) <= z_ref.shape[0]
        kstarts = [pl.multiple_of(tile * kb, kb) for _, tile, _ in todo]
        for u, (c, _, _) in enumerate(todo):
            kt = k_ref[pl.ds(kstarts[u], kb), :]
            z_ref[u] = jnp.dot(kt, qt_ref[:, c * kb:(c + 1) * kb],
                               preferred_element_type=F32)
        for u, (_, _, masked) in enumerate(todo):
            z2 = z_ref[u]
            sp2 = jnp.maximum(z2, 0.0) + jnp.log(1.0 + jnp.exp2(-jnp.abs(z2))) * LOG2E
            if masked:
                sp2 = jnp.where(causal_mask(), sp2, 0.0)
            sp_ref[u] = sp2.astype(BF16)
        for u, (c, _, masked) in enumerate(todo):
            cols = slice(c * kb, (c + 1) * kb)
            cs = jnp.dot(tri, sp_ref[u], preferred_element_type=F32)
            carry = carry_ref[:, cols]
            w = jnp.exp2(z_ref[u] - cs - carry)
            if masked:
                w = jnp.where(causal_mask(), w, 0.0)
            carry_ref[:, cols] = carry + cs[0:1, :]
            vt = vt_ref[:, pl.ds(kstarts[u], kb)]
            acc_ref[:, cols] += jnp.dot(vt, w.astype(BF16), preferred_element_type=F32)

    def unfinished(c_from=0):
        return jnp.min(carry_ref[:, c_from * kb:]) < SB_DONE_LOG2

    first = i * nchunk

    def write_out():
        o_ref[...] = (acc_ref[...].T * g_ref[...]).astype(BF16)

    has_prev = i > 0

    def kstart_of(c, d):
        return pl.multiple_of(jnp.maximum(first + c - d, 0) * kb, kb)

    slot = {(c, d): d * nchunk + c for d in (0, 1) for c in range(nchunk)}
    for (c, d), u in slot.items():
        kt = k_ref[pl.ds(kstart_of(c, d), kb), :]
        z_ref[u] = jnp.dot(kt, qt_ref[:, c * kb:(c + 1) * kb], preferred_element_type=F32)
    for (c, d), u in slot.items():
        z2 = z_ref[u]
        sp2 = jnp.maximum(z2, 0.0) + jnp.log(1.0 + jnp.exp2(-jnp.abs(z2))) * LOG2E
        if d == 0:
            sp2 = jnp.where(causal_mask(), sp2, 0.0)
        elif c == 0:
            sp2 = jnp.where(has_prev, sp2, 0.0)
        sp_ref[u] = sp2.astype(BF16)
    total = {}
    for cd, u in slot.items():
        cs = jnp.dot(tri, sp_ref[u], preferred_element_type=F32)
        z_ref[u] = z_ref[u] - cs
        total[cd] = cs[0:1, :]
    for c in range(nchunk):
        cols = slice(c * kb, (c + 1) * kb)
        w = jnp.where(causal_mask(), jnp.exp2(z_ref[slot[(c, 0)]]), 0.0)
        carry = total[(c, 0)]
        pv = jnp.dot(vt_ref[:, pl.ds(kstart_of(c, 0), kb)], w.astype(BF16),
                     preferred_element_type=F32)
        w = jnp.exp2(z_ref[slot[(c, 1)]] - carry)
        if c == 0:
            w = jnp.where(has_prev, w, 0.0)
        carry = carry + total[(c, 1)]
        pv = pv + jnp.dot(vt_ref[:, pl.ds(kstart_of(c, 1), kb)], w.astype(BF16),
                          preferred_element_type=F32)
        acc_ref[:, cols] = pv
        carry_ref[:, cols] = carry
    write_out()

    def more(state):
        d, go = state
        return jnp.logical_and(d <= first, go)

    def sweep(state):
        d, _ = state
        units([(c, first + c - d, False) for c in range(nchunk)])
        return d + 1, unfinished()

    swept = jnp.logical_and(has_prev, unfinished())
    _, go = lax.while_loop(more, sweep, (jnp.int32(2), swept))

    again = swept
    go = jnp.logical_or(go, i == 0)
    for e in range(1, nchunk):
        run = jnp.logical_and(go, has_prev) if e == 1 else go
        again = jnp.logical_or(again, run)

        @pl.when(run)
        def _():
            units([(c, c - e, False) for c in range(e, nchunk)])
        if e + 1 < nchunk:
            go = jnp.logical_and(go, unfinished(e + 1))

    pl.when(again)(write_out)


def _sb_attention(qt, k, vt, g, batch, seq):
    qb, kb = SB_QB, SB_KB
    nq = seq // qb
    tri = (jnp.arange(kb)[None, :] >= jnp.arange(kb)[:, None]).astype(BF16)
    qmap = lambda b, h, i: (b * nq + i, h)
    return pl.pallas_call(
        functools.partial(_sb_kernel, qb=qb, kb=kb),
        grid=(batch, SB_HEADS, nq),
        in_specs=[
            pl.BlockSpec((None, SB_HEAD_DIM, qb), lambda b, h, i: (b, h, i)),
            pl.BlockSpec((seq, SB_HEAD_DIM), lambda b, h, i: (b, h)),
            pl.BlockSpec((None, SB_HEAD_DIM, seq), lambda b, h, i: (b, h, 0)),
            pl.BlockSpec((qb, SB_HEAD_DIM), qmap),
            pl.BlockSpec((kb, kb), lambda b, h, i: (0, 0)),
        ],
        out_specs=pl.BlockSpec((qb, SB_HEAD_DIM), qmap),
        out_shape=jax.ShapeDtypeStruct((batch * seq, D_MODEL), BF16),
        scratch_shapes=[
            pltpu.VMEM((SB_HEAD_DIM, qb), F32),
            pltpu.VMEM((1, qb), F32),
            pltpu.VMEM((2 * (qb // kb), kb, kb), F32),
            pltpu.VMEM((2 * (qb // kb), kb, kb), BF16),
        ],
        compiler_params=pltpu.CompilerParams(
            dimension_semantics=("arbitrary", "arbitrary", "arbitrary"),
            vmem_limit_bytes=VMEM_LIMIT_BYTES),
        name="stickbreak",
    )(qt, k, vt, g, tri)


def _ret_kernel(q_ref, k_ref, v_ref, g_ref, d_ref, qd_ref, kd_ref, gl_ref, og_ref,
                o_ref, state_ref, *, tokens, blk):
    n = pl.program_id(2)

    @pl.when(n == 0)
    def _():
        state_ref[...] = jnp.zeros_like(state_ref)

    decay = d_ref[...]
    qdec = qd_ref[...]
    kdec = kd_ref[...]
    gl = gl_ref[...]
    gain = og_ref[...]
    for c in range(tokens // blk):
        sl = slice(c * blk, (c + 1) * blk)
        q = q_ref[sl, :]
        k = k_ref[sl, :]
        v = v_ref[sl, :]
        s = lax.dot_general(q, k, (((1,), (1,)), ((), ())), preferred_element_type=F32)
        o = jnp.dot((s * decay).astype(BF16), v, preferred_element_type=F32)
        state = state_ref[...]
        o = o + qdec * jnp.dot(q, state.astype(BF16), preferred_element_type=F32)
        kdt = (k.astype(F32) * kdec).T.astype(BF16)
        state_ref[...] = state * gl + jnp.dot(kdt, v, preferred_element_type=F32)
        o_ref[sl, :] = (_rms(o, gain) * g_ref[sl, :]).astype(BF16)


def _retention(rq, rk, rv, rg, ret_out_gain, batch, seq):
    blk = RET_L
    tokens = min(RET_T, seq)
    nt = seq // tokens
    log_gamma = jnp.log1p(-jnp.exp2(-5.0 - jnp.arange(RET_HEADS, dtype=F32)))
    t = jnp.arange(blk, dtype=F32)
    ct = jnp.arange(blk) // CHUNK
    dist = t[:, None] - t[None, :]
    lg = log_gamma[:, None, None]
    same = (ct[:, None] == ct[None, :])[None]
    earlier = (ct[None, :] < ct[:, None])[None]
    decay = jnp.where(same, jnp.exp(lg * jnp.abs(dist)[None]),
                      jnp.where(earlier, jnp.exp(lg * dist[None]), 0.0))
    ones = jnp.ones((1, 1, RET_DIM), F32)
    qdec = jnp.exp(log_gamma[:, None] * (t + 1.0)[None, :])[:, :, None] * ones
    kdec = jnp.exp(log_gamma[:, None] * (blk - 1.0 - t)[None, :])[:, :, None] * ones
    gl = jnp.exp(log_gamma * blk)[:, None, None] * ones
    gain = ret_out_gain.reshape(RET_HEADS, 1, RET_DIM)

    tmap = lambda b, h, n: (b * nt + n, h)
    hmap = lambda b, h, n: (h, 0, 0)
    tok_spec = pl.BlockSpec((tokens, RET_DIM), tmap)
    return pl.pallas_call(
        functools.partial(_ret_kernel, tokens=tokens, blk=blk),
        grid=(batch, RET_HEADS, nt),
        in_specs=[
            tok_spec, tok_spec, tok_spec, tok_spec,
            pl.BlockSpec((None, blk, blk), hmap),
            pl.BlockSpec((None, blk, RET_DIM), hmap),
            pl.BlockSpec((None, blk, RET_DIM), hmap),
            pl.BlockSpec((None, 1, RET_DIM), hmap),
            pl.BlockSpec((None, 1, RET_DIM), hmap),
        ],
        out_specs=tok_spec,
        out_shape=jax.ShapeDtypeStruct((batch * seq, D_MODEL), BF16),
        scratch_shapes=[pltpu.VMEM((RET_DIM, RET_DIM), F32)],
        compiler_params=pltpu.CompilerParams(
            dimension_semantics=("arbitrary", "arbitrary", "arbitrary"),
            vmem_limit_bytes=VMEM_LIMIT_BYTES),
        name="retention",
    )(rq, rk, rv, rg, decay, qdec, kdec, gl, gain)


def _out_kernel(oa_ref, ob_ref, msb_ref, mret_ref, x_ref, wsb_ref, wret_ref, wout_ref,
                o_ref):
    pa = jnp.dot(oa_ref[...], wsb_ref[...], preferred_element_type=F32)
    pb = jnp.dot(ob_ref[...], wret_ref[...], preferred_element_type=F32)
    merged = msb_ref[...] * pa + mret_ref[...] * pb
    o_ref[...] = x_ref[...] + jnp.dot(merged.astype(BF16), wout_ref[...],
                                      preferred_element_type=F32)


def _out_proj(oa, ob, msb, mret, x2, wsb, wret, wout):
    m = x2.shape[0]
    tm = OUT_TM
    row = lambda i: (i, 0)
    const = lambda i: (0, 0)
    act = pl.BlockSpec((tm, D_MODEL), row)
    wspec = pl.BlockSpec((D_MODEL, D_MODEL), const)
    return pl.pallas_call(
        _out_kernel,
        grid=(m // tm,),
        in_specs=[act, act, act, act, act, wspec, wspec, wspec],
        out_specs=act,
        out_shape=jax.ShapeDtypeStruct((m, D_MODEL), F32),
        compiler_params=pltpu.CompilerParams(
            dimension_semantics=("arbitrary",),
            vmem_limit_bytes=VMEM_LIMIT_BYTES),
        name="outproj",
    )(oa, ob, msb, mret, x2, wsb, wret, wout)


def _rope_tables(seq):
    d = RET_DIM
    inv_freq = ROPE_BASE ** (-jnp.arange(0, d, 2, dtype=F32) / d)
    ang = jnp.arange(seq, dtype=F32)[:, None] * inv_freq[None, :]
    return jnp.cos(ang), jnp.sin(ang)


def kernel(x, norm_gain, w_in, b_merge, sb_q_gain, sb_k_gain, ret_out_gain,
           w_branch_sb, w_branch_ret, w_out):
    batch, seq, d_model = x.shape
    depth = norm_gain.shape[0]
    assert d_model == D_MODEL and w_in.shape[-1] == N_SEGMENTS * D_MODEL
    assert seq % max(SB_QB, IN_TM, OUT_TM, RET_L) == 0 and seq % min(RET_T, seq) == 0
    cos, sin = _rope_tables(seq)
    x2 = x.reshape(batch * seq, D_MODEL)
    for layer in range(depth):
        qt, k, vt, sg, rq, rk, rv, rg, msb, mret = _inproj(
            x2, norm_gain[layer][None, :], w_in[layer].astype(BF16),
            sb_q_gain[layer][None, :], sb_k_gain[layer][None, :], cos, sin,
            b_merge[layer], seq)
        out_a = _sb_attention(qt, k, vt, sg, batch, seq)
        out_b = _retention(rq, rk, rv, rg, ret_out_gain[layer], batch, seq)
        x2 = _out_proj(out_a, out_b, msb, mret, x2,
                       w_branch_sb[layer].astype(BF16), w_branch_ret[layer].astype(BF16),
                       w_out[layer].astype(BF16))
    return x2.reshape(batch, seq, D_MODEL)
```

```python
import functools

import jax
import jax.numpy as jnp
from jax import lax
from jax.experimental import pallas as pl
from jax.experimental.pallas import tpu as pltpu

D_MODEL = 1024
SB_HEADS = 8
SB_HEAD_DIM = D_MODEL // SB_HEADS
RET_HEADS = 4
RET_DIM = D_MODEL // RET_HEADS
CHUNK = 64
ROPE_BASE = 10000.0
EPS = 1e-6
N_SEGMENTS = 10
LOG2E = 1.4426950408889634

VMEM_LIMIT_BYTES = 56 * 1024 * 1024

TM = 256
SB_DONE_LOG2 = 160.0

F32 = jnp.float32
BF16 = jnp.bfloat16


def _rms(x, gain):
    y = x * lax.rsqrt(jnp.mean(x * x, axis=-1, keepdims=True) + EPS)
    return y * gain


def _silu(g):
    return g * jax.nn.sigmoid(g)


def _inproj_kernel(x_ref, ng_ref, w_ref, qg_ref, kg_ref, cos_ref, sin_ref, bm_ref,
                   qt_ref, k_ref, vt_ref, sg_ref, rq_ref, rk_ref, rv_ref, rg_ref,
                   msb_ref, mret_ref):
    hb = _rms(x_ref[...], ng_ref[...]).astype(BF16)

    def seg(s):
        return jnp.dot(hb, w_ref[:, s * D_MODEL:(s + 1) * D_MODEL],
                       preferred_element_type=F32)

    def heads(p, out_ref, gain_ref=None, post_scale=None, transpose=False):
        for h in range(SB_HEADS):
            sl = slice(h * SB_HEAD_DIM, (h + 1) * SB_HEAD_DIM)
            y = p[:, sl]
            if gain_ref is not None:
                y = _rms(y, gain_ref[...])
            if post_scale is not None:
                y = y * post_scale
            if transpose:
                out_ref[sl, :] = y.T.astype(BF16)
            else:
                out_ref[:, sl] = y.astype(BF16)

    def rotary(p, out_ref, post_scale):
        cos = cos_ref[...]
        sin = sin_ref[...]
        half = RET_DIM // 2
        for h in range(RET_HEADS):
            t1 = p[:, h * RET_DIM:h * RET_DIM + half]
            t2 = p[:, h * RET_DIM + half:(h + 1) * RET_DIM]
            o1 = t1 * cos - t2 * sin
            o2 = t1 * sin + t2 * cos
            if post_scale is not None:
                o1 = o1 * post_scale
                o2 = o2 * post_scale
            out_ref[:, h * RET_DIM:h * RET_DIM + half] = o1.astype(BF16)
            out_ref[:, h * RET_DIM + half:(h + 1) * RET_DIM] = o2.astype(BF16)

    heads(seg(0), qt_ref, qg_ref, (SB_HEAD_DIM ** -0.5) * LOG2E, transpose=True)
    heads(seg(1), k_ref, kg_ref)
    heads(seg(2), vt_ref, transpose=True)
    sg_ref[...] = _silu(seg(3))
    rotary(seg(4), rq_ref, None)
    rotary(seg(5), rk_ref, RET_DIM ** -0.5)
    rv_ref[...] = seg(6).astype(BF16)
    rg_ref[...] = _silu(seg(7))
    msb_ref[...] = jax.nn.sigmoid(seg(8) + bm_ref[0:1, :])
    mret_ref[...] = jax.nn.sigmoid(seg(9) + bm_ref[1:2, :])


def _inproj(x2, norm_gain, w_in_bf, q_gain, k_gain, cos, sin, b_merge, seq):
    m = x2.shape[0]
    pos_blocks = seq // TM
    row = lambda i: (i, 0)
    const = lambda i: (0, 0)
    pos = lambda i: (i % pos_blocks, 0)
    bf_out = jax.ShapeDtypeStruct((m, D_MODEL), BF16)
    f32_out = jax.ShapeDtypeStruct((m, D_MODEL), F32)
    out_spec = pl.BlockSpec((TM, D_MODEL), row)
    t_out = jax.ShapeDtypeStruct((m // seq, D_MODEL, seq), BF16)
    t_spec = pl.BlockSpec((None, D_MODEL, TM), lambda i: (i // pos_blocks, 0, i % pos_blocks))
    return pl.pallas_call(
        _inproj_kernel,
        grid=(m // TM,),
        in_specs=[
            pl.BlockSpec((TM, D_MODEL), row),
            pl.BlockSpec((1, D_MODEL), const),
            pl.BlockSpec((D_MODEL, N_SEGMENTS * D_MODEL), const),
            pl.BlockSpec((1, SB_HEAD_DIM), const),
            pl.BlockSpec((1, SB_HEAD_DIM), const),
            pl.BlockSpec((TM, RET_DIM // 2), pos),
            pl.BlockSpec((TM, RET_DIM // 2), pos),
            pl.BlockSpec((2, D_MODEL), const),
        ],
        out_specs=[t_spec, out_spec, t_spec] + [out_spec] * 7,
        out_shape=[t_out, bf_out, t_out, f32_out, bf_out, bf_out, bf_out,
                   f32_out, f32_out, f32_out],
        compiler_params=pltpu.CompilerParams(
            dimension_semantics=("arbitrary",),
            vmem_limit_bytes=VMEM_LIMIT_BYTES),
        name="inproj",
    )(x2, norm_gain, w_in_bf, q_gain, k_gain, cos, sin, b_merge)


def _softplus2(z2):
    return jnp.maximum(z2, 0.0) + jnp.log(1.0 + jnp.exp2(-jnp.abs(z2))) * LOG2E


def _bm_kernel(qt_ref, kc_ref, kp_ref, vtc_ref, vtp_ref, sg_ref,
               rq_ref, rk_ref, rv_ref, rg_ref, dec_ref, qd_ref, kd_ref, gl_ref, og_ref,
               msb_ref, mret_ref, x_ref, wsb_ref, wret_ref, wout_ref, tri_ref,
               k_hbm, vt_hbm,
               o_ref,
               oa_ref, ob_ref, mg_ref, state_ref, z_ref, sp_ref, acc_ref, carry_ref, kbuf_ref,
               vbuf_ref, *, tiles_per_seq):
    m = pl.program_id(0)
    cur = jnp.minimum(m, pl.num_programs(0) - 2)
    b = cur // tiles_per_seq
    t = cur % tiles_per_seq
    has_prev = t > 0
    slot = m % 2
    tri = tri_ref[...]

    @pl.when(m == 0)
    def _():
        oa_ref[...] = jnp.zeros_like(oa_ref)
        ob_ref[...] = jnp.zeros_like(ob_ref)

    def head_cols(h):
        return slice(h * SB_HEAD_DIM, (h + 1) * SB_HEAD_DIM)

    def causal_mask():
        row = lax.broadcasted_iota(jnp.int32, (TM, TM), 0)
        col = lax.broadcasted_iota(jnp.int32, (TM, TM), 1)
        return row < col

    def write_head(h):
        oa_ref[slot, :, head_cols(h)] = (
            acc_ref[h].T * sg_ref[:, head_cols(h)]).astype(BF16)

    n_piece = D_MODEL // TM

    def merged_piece(j):
        cols = slice(j * TM, (j + 1) * TM)
        pa = jnp.dot(oa_ref[1 - slot], wsb_ref[:, cols], preferred_element_type=F32)
        pb = jnp.dot(ob_ref[1 - slot], wret_ref[:, cols], preferred_element_type=F32)
        mg_ref[:, cols] = (msb_ref[:, cols] * pa + mret_ref[:, cols] * pb).astype(BF16)

    def out_piece(j):
        cols = slice(j * TM, (j + 1) * TM)
        o_ref[:, cols] = x_ref[:, cols] + jnp.dot(mg_ref[...], wout_ref[:, cols],
                                                  preferred_element_type=F32)

    def retention_head(h):
        cols = slice(h * RET_DIM, (h + 1) * RET_DIM)
        q = rq_ref[:, cols]
        k = rk_ref[:, cols]
        v = rv_ref[:, cols]
        s = lax.dot_general(q, k, (((1,), (1,)), ((), ())), preferred_element_type=F32)
        o = jnp.dot((s * dec_ref[h]).astype(BF16), v, preferred_element_type=F32)
        state = jnp.where(has_prev, state_ref[h], 0.0)
        o = o + qd_ref[h] * jnp.dot(q, state.astype(BF16), preferred_element_type=F32)
        kdt = (k.astype(F32) * kd_ref[h]).T.astype(BF16)
        state_ref[h] = state * gl_ref[h] + jnp.dot(kdt, v, preferred_element_type=F32)
        ob_ref[slot, :, cols] = (_rms(o, og_ref[h]) * rg_ref[:, cols]).astype(BF16)

    units = [(h, d) for d in (0, 1) for h in range(SB_HEADS)]
    n_units = len(units)
    for u, (h, d) in enumerate(units):
        kt = (kc_ref if d == 0 else kp_ref)[:, head_cols(h)]
        z_ref[u] = jnp.dot(kt, qt_ref[head_cols(h), :], preferred_element_type=F32)
    for u, (h, d) in enumerate(units):
        sp2 = _softplus2(z_ref[u])
        sp2 = jnp.where(causal_mask() if d == 0 else has_prev, sp2, 0.0)
        sp_ref[u] = sp2.astype(BF16)
        if (u + 1) % (n_units // n_piece) == 0:
            merged_piece(u // (n_units // n_piece))
    total = []
    for u in range(n_units):
        cs = jnp.dot(tri, sp_ref[u], preferred_element_type=F32)
        z_ref[u] = z_ref[u] - cs
        total.append(cs[0:1, :])
    for h in range(SB_HEADS):
        w = jnp.where(causal_mask(), jnp.exp2(z_ref[h]), 0.0)
        carry = total[h]
        pv = jnp.dot(vtc_ref[head_cols(h), :], w.astype(BF16), preferred_element_type=F32)
        w = jnp.where(has_prev, jnp.exp2(z_ref[SB_HEADS + h] - carry), 0.0)
        pv = pv + jnp.dot(vtp_ref[head_cols(h), :], w.astype(BF16),
                          preferred_element_type=F32)
        acc_ref[h] = pv
        carry_ref[h] = carry + total[SB_HEADS + h]
        write_head(h)
        if (h + 1) % (SB_HEADS // n_piece) == 0:
            out_piece(h // (SB_HEADS // n_piece))
    for h in range(RET_HEADS):
        retention_head(h)

    def unfinished(h):
        return jnp.min(carry_ref[h]) < SB_DONE_LOG2

    @pl.when(jnp.logical_and(t >= 2, jnp.min(carry_ref[...]) < SB_DONE_LOG2))
    def _():
        for h in range(SB_HEADS):
            def more(state):
                d, go = state
                return jnp.logical_and(d <= t, go)

            def sweep(state, h=h):
                d, _ = state
                tile = t - d
                row0 = pl.multiple_of((b * tiles_per_seq + tile) * TM, TM)
                col0 = pl.multiple_of(tile * TM, TM)
                pltpu.sync_copy(k_hbm.at[pl.ds(row0, TM), head_cols(h)], kbuf_ref)
                pltpu.sync_copy(vt_hbm.at[b, head_cols(h), pl.ds(col0, TM)], vbuf_ref)
                z2 = jnp.dot(kbuf_ref[...], qt_ref[head_cols(h), :],
                             preferred_element_type=F32)
                cs = jnp.dot(tri, _softplus2(z2).astype(BF16), preferred_element_type=F32)
                carry = carry_ref[h]
                w = jnp.exp2(z2 - cs - carry)
                carry_ref[h] = carry + cs[0:1, :]
                acc_ref[h] += jnp.dot(vbuf_ref[...], w.astype(BF16),
                                      preferred_element_type=F32)
                return d + 1, unfinished(h)

            lax.while_loop(more, sweep, (jnp.int32(2), unfinished(h)))
            write_head(h)


def _retention_tables():
    log_gamma = jnp.log1p(-jnp.exp2(-5.0 - jnp.arange(RET_HEADS, dtype=F32)))
    t = jnp.arange(TM, dtype=F32)
    ct = jnp.arange(TM) // CHUNK
    dist = t[:, None] - t[None, :]
    lg = log_gamma[:, None, None]
    same = (ct[:, None] == ct[None, :])[None]
    earlier = (ct[None, :] < ct[:, None])[None]
    decay = jnp.where(same, jnp.exp(lg * jnp.abs(dist)[None]),
                      jnp.where(earlier, jnp.exp(lg * dist[None]), 0.0))
    ones = jnp.ones((1, 1, RET_DIM), F32)
    qdec = jnp.exp(log_gamma[:, None] * (t + 1.0)[None, :])[:, :, None] * ones
    kdec = jnp.exp(log_gamma[:, None] * (TM - 1.0 - t)[None, :])[:, :, None] * ones
    gl = jnp.exp(log_gamma * TM)[:, None, None] * ones
    return decay, qdec, kdec, gl


def _branches_merge(qt, k, vt, sg, rq, rk, rv, rg, msb, mret, x2, ret_out_gain,
                    wsb, wret, wout, batch, seq):
    m = x2.shape[0]
    n_tiles = m // TM
    tiles_per_seq = seq // TM
    last = n_tiles - 1
    tri = (jnp.arange(TM)[None, :] >= jnp.arange(TM)[:, None]).astype(BF16)
    decay, qdec, kdec, gl = _retention_tables()
    gain = ret_out_gain.reshape(RET_HEADS, 1, RET_DIM)

    cur = lambda i: jnp.minimum(i, last)
    row_cur = lambda i: (cur(i), 0)
    row_prev_tile = lambda i: (jnp.maximum(cur(i) - 1, 0), 0)
    row_lag = lambda i: (jnp.maximum(i - 1, 0), 0)
    t_cur = lambda i: (cur(i) // tiles_per_seq, 0, cur(i) % tiles_per_seq)
    t_prev = lambda i: (cur(i) // tiles_per_seq, 0,
                        jnp.maximum(cur(i) % tiles_per_seq - 1, 0))
    full2 = lambda i: (0, 0)
    full3 = lambda i: (0, 0, 0)

    rows = lambda imap: pl.BlockSpec((TM, D_MODEL), imap)
    trans = lambda imap: pl.BlockSpec((None, D_MODEL, TM), imap)
    table = lambda a: pl.BlockSpec(a.shape, full3)
    weight = pl.BlockSpec((D_MODEL, D_MODEL), full2)
    hbm = pl.BlockSpec(memory_space=pl.ANY)
    return pl.pallas_call(
        functools.partial(_bm_kernel, tiles_per_seq=tiles_per_seq),
        grid=(n_tiles + 1,),
        in_specs=[
            trans(t_cur), rows(row_cur), rows(row_prev_tile), trans(t_cur), trans(t_prev),
            rows(row_cur),
            rows(row_cur), rows(row_cur), rows(row_cur), rows(row_cur),
            table(decay), table(qdec), table(kdec), table(gl), table(gain),
            rows(row_lag), rows(row_lag), rows(row_lag), weight, weight, weight,
            pl.BlockSpec((TM, TM), full2),
            hbm, hbm,
        ],
        out_specs=rows(row_lag),
        out_shape=jax.ShapeDtypeStruct((m, D_MODEL), F32),
        scratch_shapes=[
            pltpu.VMEM((2, TM, D_MODEL), BF16),
            pltpu.VMEM((2, TM, D_MODEL), BF16),
            pltpu.VMEM((TM, D_MODEL), BF16),
            pltpu.VMEM((RET_HEADS, RET_DIM, RET_DIM), F32),
            pltpu.VMEM((2 * SB_HEADS, TM, TM), F32),
            pltpu.VMEM((2 * SB_HEADS, TM, TM), BF16),
            pltpu.VMEM((SB_HEADS, SB_HEAD_DIM, TM), F32),
            pltpu.VMEM((SB_HEADS, 1, TM), F32),
            pltpu.VMEM((TM, SB_HEAD_DIM), BF16),
            pltpu.VMEM((SB_HEAD_DIM, TM), BF16),
        ],
        compiler_params=pltpu.CompilerParams(
            dimension_semantics=("arbitrary",),
            vmem_limit_bytes=VMEM_LIMIT_BYTES),
        name="branches_merge",
    )(qt, k, k, vt, vt, sg, rq, rk, rv, rg, decay, qdec, kdec, gl, gain,
      msb, mret, x2, wsb, wret, wout, tri, k, vt)


def _rope_tables(seq):
    d = RET_DIM
    inv_freq = ROPE_BASE ** (-jnp.arange(0, d, 2, dtype=F32) / d)
    ang = jnp.arange(seq, dtype=F32)[:, None] * inv_freq[None, :]
    return jnp.cos(ang), jnp.sin(ang)


def kernel(x, norm_gain, w_in, b_merge, sb_q_gain, sb_k_gain, ret_out_gain,
           w_branch_sb, w_branch_ret, w_out):
    batch, seq, d_model = x.shape
    depth = norm_gain.shape[0]
    assert d_model == D_MODEL and w_in.shape[-1] == N_SEGMENTS * D_MODEL
    assert seq % TM == 0 and TM % CHUNK == 0
    cos, sin = _rope_tables(seq)
    x2 = x.reshape(batch * seq, D_MODEL)
    for layer in range(depth):
        qt, k, vt, sg, rq, rk, rv, rg, msb, mret = _inproj(
            x2, norm_gain[layer][None, :], w_in[layer].astype(BF16),
            sb_q_gain[layer][None, :], sb_k_gain[layer][None, :], cos, sin,
            b_merge[layer], seq)
        x2 = _branches_merge(
            qt, k, vt, sg, rq, rk, rv, rg, msb, mret, x2, ret_out_gain[layer],
            w_branch_sb[layer].astype(BF16), w_branch_ret[layer].astype(BF16),
            w_out[layer].astype(BF16), batch, seq)
    return x2.reshape(batch, seq, D_MODEL)
```

```python
import functools

import jax
import jax.numpy as jnp
from jax import lax
from jax.experimental import pallas as pl
from jax.experimental.pallas import tpu as pltpu

D_MODEL = 1024
SB_HEADS = 8
SB_HEAD_DIM = D_MODEL // SB_HEADS
RET_HEADS = 4
RET_DIM = D_MODEL // RET_HEADS
CHUNK = 64
ROPE_BASE = 10000.0
EPS = 1e-6
N_SEGMENTS = 10
LOG2E = 1.4426950408889634

VMEM_LIMIT_BYTES = 56 * 1024 * 1024

TM = 256
SB_DONE_LOG2 = 160.0

F32 = jnp.float32
BF16 = jnp.bfloat16


def _rms(x, gain):
    y = x * lax.rsqrt(jnp.mean(x * x, axis=-1, keepdims=True) + EPS)
    return y * gain


def _silu(g):
    return g * jax.nn.sigmoid(g)


def _inproj_kernel(x_ref, ng_ref, w_ref, qg_ref, kg_ref, cos_ref, sin_ref, bm_ref,
                   qt_ref, k_ref, vt_ref, sg_ref, rq_ref, rk_ref, rv_ref, rg_ref,
                   msb_ref, mret_ref):
    hb = _rms(x_ref[...], ng_ref[...]).astype(BF16)

    def seg(s):
        return jnp.dot(hb, w_ref[:, s * D_MODEL:(s + 1) * D_MODEL],
                       preferred_element_type=F32)

    def heads(p, out_ref, gain_ref=None, post_scale=None, transpose=False):
        for h in range(SB_HEADS):
            sl = slice(h * SB_HEAD_DIM, (h + 1) * SB_HEAD_DIM)
            y = p[:, sl]
            if gain_ref is not None:
                y = _rms(y, gain_ref[...])
            if post_scale is not None:
                y = y * post_scale
            if transpose:
                out_ref[sl, :] = y.T.astype(BF16)
            else:
                out_ref[:, sl] = y.astype(BF16)

    def rotary(p, out_ref, post_scale):
        cos = cos_ref[...]
        sin = sin_ref[...]
        half = RET_DIM // 2
        for h in range(RET_HEADS):
            t1 = p[:, h * RET_DIM:h * RET_DIM + half]
            t2 = p[:, h * RET_DIM + half:(h + 1) * RET_DIM]
            o1 = t1 * cos - t2 * sin
            o2 = t1 * sin + t2 * cos
            if post_scale is not None:
                o1 = o1 * post_scale
                o2 = o2 * post_scale
            out_ref[:, h * RET_DIM:h * RET_DIM + half] = o1.astype(BF16)
            out_ref[:, h * RET_DIM + half:(h + 1) * RET_DIM] = o2.astype(BF16)

    heads(seg(0), qt_ref, qg_ref, (SB_HEAD_DIM ** -0.5) * LOG2E, transpose=True)
    heads(seg(1), k_ref, kg_ref)
    heads(seg(2), vt_ref, transpose=True)
    sg_ref[...] = _silu(seg(3))
    rotary(seg(4), rq_ref, None)
    rotary(seg(5), rk_ref, RET_DIM ** -0.5)
    rg_ref[...] = _silu(seg(7))
    msb_ref[...] = jax.nn.sigmoid(seg(8) + bm_ref[0:1, :])
    mret_ref[...] = jax.nn.sigmoid(seg(9) + bm_ref[1:2, :])
    rv_ref[...] = seg(6).astype(BF16)


def _inproj(x2, norm_gain, w_in_bf, q_gain, k_gain, cos, sin, b_merge, seq):
    m = x2.shape[0]
    pos_blocks = seq // TM
    row = lambda i: (i, 0)
    const = lambda i: (0, 0)
    pos = lambda i: (i % pos_blocks, 0)
    bf_out = jax.ShapeDtypeStruct((m, D_MODEL), BF16)
    f32_out = jax.ShapeDtypeStruct((m, D_MODEL), F32)
    out_spec = pl.BlockSpec((TM, D_MODEL), row)
    t_out = jax.ShapeDtypeStruct((m // seq, D_MODEL, seq), BF16)
    t_spec = pl.BlockSpec((None, D_MODEL, TM), lambda i: (i // pos_blocks, 0, i % pos_blocks))
    return pl.pallas_call(
        _inproj_kernel,
        grid=(m // TM,),
        in_specs=[
            pl.BlockSpec((TM, D_MODEL), row),
            pl.BlockSpec((1, D_MODEL), const),
            pl.BlockSpec((D_MODEL, N_SEGMENTS * D_MODEL), const),
            pl.BlockSpec((1, SB_HEAD_DIM), const),
            pl.BlockSpec((1, SB_HEAD_DIM), const),
            pl.BlockSpec((TM, RET_DIM // 2), pos),
            pl.BlockSpec((TM, RET_DIM // 2), pos),
            pl.BlockSpec((2, D_MODEL), const),
        ],
        out_specs=[t_spec, out_spec, t_spec] + [out_spec] * 7,
        out_shape=[t_out, bf_out, t_out, f32_out, bf_out, bf_out, bf_out,
                   f32_out, f32_out, f32_out],
        compiler_params=pltpu.CompilerParams(
            dimension_semantics=("arbitrary",),
            vmem_limit_bytes=VMEM_LIMIT_BYTES),
        name="inproj",
    )(x2, norm_gain, w_in_bf, q_gain, k_gain, cos, sin, b_merge)


def _softplus2(z2):
    return jnp.maximum(z2, 0.0) + jnp.log(1.0 + jnp.exp2(-jnp.abs(z2))) * LOG2E


def _bm_kernel(qt_ref, kc_ref, kp_ref, vtc_ref, vtp_ref, sg_ref,
               rq_ref, rk_ref, rv_ref, rg_ref, dec_ref, qd_ref, kd_ref, gl_ref, og_ref,
               msb_ref, mret_ref, x_ref, wsb_ref, wret_ref, wout_ref, tri_ref,
               k_hbm, vt_hbm,
               o_ref,
               oa_ref, ob_ref, mg_ref, state_ref, rs_ref, rq_state_ref, z_ref, sp_ref, acc_ref,
               carry_ref, kbuf_ref, vbuf_ref, *, tiles_per_seq):
    m = pl.program_id(0)
    cur = jnp.minimum(m, pl.num_programs(0) - 2)
    b = cur // tiles_per_seq
    t = cur % tiles_per_seq
    has_prev = t > 0
    slot = m % 2
    tri = tri_ref[...]

    @pl.when(m == 0)
    def _():
        def zero_rows(r, _):
            rows = pl.ds(pl.multiple_of(r * 16, 16), 16)
            oa_ref[1, rows, :] = jnp.zeros((16, D_MODEL), BF16)
            ob_ref[1, rows, :] = jnp.zeros((16, D_MODEL), BF16)
            return 0

        lax.fori_loop(0, TM // 16, zero_rows, 0)

    def head_cols(h):
        return slice(h * SB_HEAD_DIM, (h + 1) * SB_HEAD_DIM)

    def causal_mask():
        row = lax.broadcasted_iota(jnp.int32, (TM, TM), 0)
        col = lax.broadcasted_iota(jnp.int32, (TM, TM), 1)
        return row < col

    def write_head(h):
        oa_ref[slot, :, head_cols(h)] = (
            acc_ref[h].T * sg_ref[:, head_cols(h)]).astype(BF16)

    n_piece = D_MODEL // TM

    def merged_piece(j):
        cols = slice(j * TM, (j + 1) * TM)
        pa = jnp.dot(oa_ref[1 - slot], wsb_ref[:, cols], preferred_element_type=F32)
        pb = jnp.dot(ob_ref[1 - slot], wret_ref[:, cols], preferred_element_type=F32)
        mg_ref[:, cols] = (msb_ref[:, cols] * pa + mret_ref[:, cols] * pb).astype(BF16)

    def out_piece(j):
        cols = slice(j * TM, (j + 1) * TM)
        o_ref[:, cols] = x_ref[:, cols] + jnp.dot(mg_ref[...], wout_ref[:, cols],
                                                  preferred_element_type=F32)

    def ret_cols(h):
        return slice(h * RET_DIM, (h + 1) * RET_DIM)

    def retention_scores(h):
        q = rq_ref[:, ret_cols(h)]
        k = rk_ref[:, ret_cols(h)]
        v = rv_ref[:, ret_cols(h)]
        rs_ref[h] = lax.dot_general(q, k, (((1,), (1,)), ((), ())),
                                    preferred_element_type=F32)
        state = jnp.where(has_prev, state_ref[h], 0.0)
        rq_state_ref[h] = jnp.dot(q, state.astype(BF16), preferred_element_type=F32)
        kdt = (k.astype(F32) * kd_ref[h]).T.astype(BF16)
        state_ref[h] = state * gl_ref[h] + jnp.dot(kdt, v, preferred_element_type=F32)

    def retention_out(h):
        p = (rs_ref[h] * dec_ref[h]).astype(BF16)
        o = jnp.dot(p, rv_ref[:, ret_cols(h)], preferred_element_type=F32)
        o = o + qd_ref[h] * rq_state_ref[h]
        ob_ref[slot, :, ret_cols(h)] = (
            _rms(o, og_ref[h]) * rg_ref[:, ret_cols(h)]).astype(BF16)

    units = [(h, d) for d in (0, 1) for h in range(SB_HEADS)]
    n_units = len(units)
    for u, (h, d) in enumerate(units):
        kt = (kc_ref if d == 0 else kp_ref)[:, head_cols(h)]
        z_ref[u] = jnp.dot(kt, qt_ref[head_cols(h), :], preferred_element_type=F32)
        if (u + 1) % (n_units // RET_HEADS) == 0:
            retention_scores(u // (n_units // RET_HEADS))
    for u, (h, d) in enumerate(units):
        sp2 = _softplus2(z_ref[u])
        sp2 = jnp.where(causal_mask() if d == 0 else has_prev, sp2, 0.0)
        sp_ref[u] = sp2.astype(BF16)
        if (u + 1) % (n_units // n_piece) == 0:
            merged_piece(u // (n_units // n_piece))
    total = {}

    def cumsum(h):
        for u in (h, SB_HEADS + h):
            cs = jnp.dot(tri, sp_ref[u], preferred_element_type=F32)
            z_ref[u] = z_ref[u] - cs
            total[u] = cs[0:1, :]

    ahead = 2
    for h in range(ahead):
        cumsum(h)
    for h in range(SB_HEADS):
        if h + ahead < SB_HEADS:
            cumsum(h + ahead)
        w = jnp.where(causal_mask(), jnp.exp2(z_ref[h]), 0.0)
        carry = total[h]
        pv = jnp.dot(vtc_ref[head_cols(h), :], w.astype(BF16), preferred_element_type=F32)
        w = jnp.where(has_prev, jnp.exp2(z_ref[SB_HEADS + h] - carry), 0.0)
        pv = pv + jnp.dot(vtp_ref[head_cols(h), :], w.astype(BF16),
                          preferred_element_type=F32)
        acc_ref[h] = pv
        carry_ref[h] = carry + total[SB_HEADS + h]
        write_head(h)
        if (h + 1) % (SB_HEADS // n_piece) == 0:
            out_piece(h // (SB_HEADS // n_piece))
        if (h + 1) % (SB_HEADS // RET_HEADS) == 0:
            retention_out(h // (SB_HEADS // RET_HEADS))

    def unfinished(h):
        return jnp.min(carry_ref[h]) < SB_DONE_LOG2

    @pl.when(jnp.logical_and(t >= 2, jnp.min(carry_ref[...]) < SB_DONE_LOG2))
    def _():
        for h in range(SB_HEADS):
            def more(state):
                d, go = state
                return jnp.logical_and(d <= t, go)

            def sweep(state, h=h):
                d, _ = state
                tile = t - d
                row0 = pl.multiple_of((b * tiles_per_seq + tile) * TM, TM)
                col0 = pl.multiple_of(tile * TM, TM)
                pltpu.sync_copy(k_hbm.at[pl.ds(row0, TM), head_cols(h)], kbuf_ref)
                pltpu.sync_copy(vt_hbm.at[b, head_cols(h), pl.ds(col0, TM)], vbuf_ref)
                z2 = jnp.dot(kbuf_ref[...], qt_ref[head_cols(h), :],
                             preferred_element_type=F32)
                cs = jnp.dot(tri, _softplus2(z2).astype(BF16), preferred_element_type=F32)
                carry = carry_ref[h]
                w = jnp.exp2(z2 - cs - carry)
                carry_ref[h] = carry + cs[0:1, :]
                acc_ref[h] += jnp.dot(vbuf_ref[...], w.astype(BF16),
                                      preferred_element_type=F32)
                return d + 1, unfinished(h)

            lax.while_loop(more, sweep, (jnp.int32(2), unfinished(h)))
            write_head(h)


def _retention_tables():
    log_gamma = jnp.log1p(-jnp.exp2(-5.0 - jnp.arange(RET_HEADS, dtype=F32)))
    t = jnp.arange(TM, dtype=F32)
    ct = jnp.arange(TM) // CHUNK
    dist = t[:, None] - t[None, :]
    lg = log_gamma[:, None, None]
    same = (ct[:, None] == ct[None, :])[None]
    earlier = (ct[None, :] < ct[:, None])[None]
    decay = jnp.where(same, jnp.exp(lg * jnp.abs(dist)[None]),
                      jnp.where(earlier, jnp.exp(lg * dist[None]), 0.0))
    ones = jnp.ones((1, 1, RET_DIM), F32)
    qdec = jnp.exp(log_gamma[:, None] * (t + 1.0)[None, :])[:, :, None] * ones
    kdec = jnp.exp(log_gamma[:, None] * (TM - 1.0 - t)[None, :])[:, :, None] * ones
    gl = jnp.exp(log_gamma * TM)[:, None, None] * ones
    return decay, qdec, kdec, gl


def _branches_merge(qt, k, vt, sg, rq, rk, rv, rg, msb, mret, x2, ret_out_gain,
                    wsb, wret, wout, batch, seq):
    m = x2.shape[0]
    n_tiles = m // TM
    tiles_per_seq = seq // TM
    last = n_tiles - 1
    tri = (jnp.arange(TM)[None, :] >= jnp.arange(TM)[:, None]).astype(BF16)
    decay, qdec, kdec, gl = _retention_tables()
    gain = ret_out_gain.reshape(RET_HEADS, 1, RET_DIM)

    cur = lambda i: jnp.minimum(i, last)
    row_cur = lambda i: (cur(i), 0)
    row_prev_tile = lambda i: (jnp.maximum(cur(i) - 1, 0), 0)
    row_lag = lambda i: (jnp.maximum(i - 1, 0), 0)
    t_cur = lambda i: (cur(i) // tiles_per_seq, 0, cur(i) % tiles_per_seq)
    t_prev = lambda i: (cur(i) // tiles_per_seq, 0,
                        jnp.maximum(cur(i) % tiles_per_seq - 1, 0))
    full2 = lambda i: (0, 0)
    full3 = lambda i: (0, 0, 0)

    rows = lambda imap: pl.BlockSpec((TM, D_MODEL), imap)
    trans = lambda imap: pl.BlockSpec((None, D_MODEL, TM), imap)
    table = lambda a: pl.BlockSpec(a.shape, full3)
    weight = pl.BlockSpec((D_MODEL, D_MODEL), full2)
    hbm = pl.BlockSpec(memory_space=pl.ANY)
    return pl.pallas_call(
        functools.partial(_bm_kernel, tiles_per_seq=tiles_per_seq),
        grid=(n_tiles + 1,),
        in_specs=[
            trans(t_cur), rows(row_cur), rows(row_prev_tile), trans(t_cur), trans(t_prev),
            rows(row_cur),
            rows(row_cur), rows(row_cur), rows(row_cur), rows(row_cur),
            table(decay), table(qdec), table(kdec), table(gl), table(gain),
            rows(row_lag), rows(row_lag), rows(row_lag), weight, weight, weight,
            pl.BlockSpec((TM, TM), full2),
            hbm, hbm,
        ],
        out_specs=rows(row_lag),
        out_shape=jax.ShapeDtypeStruct((m, D_MODEL), F32),
        scratch_shapes=[
            pltpu.VMEM((2, TM, D_MODEL), BF16),
            pltpu.VMEM((2, TM, D_MODEL), BF16),
            pltpu.VMEM((TM, D_MODEL), BF16),
            pltpu.VMEM((RET_HEADS, RET_DIM, RET_DIM), F32),
            pltpu.VMEM((RET_HEADS, TM, TM), F32),
            pltpu.VMEM((RET_HEADS, TM, RET_DIM), F32),
            pltpu.VMEM((2 * SB_HEADS, TM, TM), F32),
            pltpu.VMEM((2 * SB_HEADS, TM, TM), BF16),
            pltpu.VMEM((SB_HEADS, SB_HEAD_DIM, TM), F32),
            pltpu.VMEM((SB_HEADS, 1, TM), F32),
            pltpu.VMEM((TM, SB_HEAD_DIM), BF16),
            pltpu.VMEM((SB_HEAD_DIM, TM), BF16),
        ],
        compiler_params=pltpu.CompilerParams(
            dimension_semantics=("arbitrary",),
            vmem_limit_bytes=VMEM_LIMIT_BYTES),
        name="branches_merge",
    )(qt, k, k, vt, vt, sg, rq, rk, rv, rg, decay, qdec, kdec, gl, gain,
      msb, mret, x2, wsb, wret, wout, tri, k, vt)


def _rope_tables(seq):
    d = RET_DIM
    inv_freq = ROPE_BASE ** (-jnp.arange(0, d, 2, dtype=F32) / d)
    ang = jnp.arange(seq, dtype=F32)[:, None] * inv_freq[None, :]
    return jnp.cos(ang), jnp.sin(ang)


def kernel(x, norm_gain, w_in, b_merge, sb_q_gain, sb_k_gain, ret_out_gain,
           w_branch_sb, w_branch_ret, w_out):
    batch, seq, d_model = x.shape
    depth = norm_gain.shape[0]
    assert d_model == D_MODEL and w_in.shape[-1] == N_SEGMENTS * D_MODEL
    assert seq % TM == 0 and TM % CHUNK == 0
    cos, sin = _rope_tables(seq)
    x2 = x.reshape(batch * seq, D_MODEL)
    for layer in range(depth):
        qt, k, vt, sg, rq, rk, rv, rg, msb, mret = _inproj(
            x2, norm_gain[layer][None, :], w_in[layer].astype(BF16),
            sb_q_gain[layer][None, :], sb_k_gain[layer][None, :], cos, sin,
            b_merge[layer], seq)
        x2 = _branches_merge(
            qt, k, vt, sg, rq, rk, rv, rg, msb, mret, x2, ret_out_gain[layer],
            w_branch_sb[layer].astype(BF16), w_branch_ret[layer].astype(BF16),
            w_out[layer].astype(BF16), batch, seq)
    return x2.reshape(batch, seq, D_MODEL)
```

```python
import functools

import numpy as np
import jax
import jax.numpy as jnp
from jax import lax
from jax.experimental import pallas as pl
from jax.experimental.pallas import tpu as pltpu

D_MODEL = 1024
SB_HEADS = 8
SB_HEAD_DIM = D_MODEL // SB_HEADS
RET_HEADS = 4
RET_DIM = D_MODEL // RET_HEADS
CHUNK = 64
ROPE_BASE = 10000.0
EPS = 1e-6
N_SEGMENTS = 10
LOG2E = 1.4426950408889634

VMEM_LIMIT_BYTES = 56 * 1024 * 1024

IN_TM = 256
TM = 256
SB_DONE_LOG2 = 160.0

F32 = jnp.float32
BF16 = jnp.bfloat16


def _rms(x, gain):
    y = x * lax.rsqrt(jnp.mean(x * x, axis=-1, keepdims=True) + EPS)
    return y * gain


def _silu(g):
    return g * jax.nn.sigmoid(g)


def _inproj_kernel(x_ref, ng_ref, w_ref, qg_ref, kg_ref, cosa_ref, sina_ref, cosb_ref,
                   sinb_ref, bm_ref,
                   qt_ref, k_ref, vt_ref, sg_ref, rq_ref, rk_ref, rv_ref, rg_ref,
                   msb_ref, mret_ref):
    hb = _rms(x_ref[...], ng_ref[...]).astype(BF16)

    def seg(s):
        return jnp.dot(hb, w_ref[:, s * D_MODEL:(s + 1) * D_MODEL],
                       preferred_element_type=F32)

    def heads(p, out_ref, gain_ref=None, post_scale=None, transpose=False):
        for h in range(SB_HEADS):
            sl = slice(h * SB_HEAD_DIM, (h + 1) * SB_HEAD_DIM)
            y = p[:, sl]
            if gain_ref is not None:
                y = _rms(y, gain_ref[...])
            if post_scale is not None:
                y = y * post_scale
            if transpose:
                out_ref[sl, :] = y.T.astype(BF16)
            else:
                out_ref[:, sl] = y.astype(BF16)

    ca, sa = cosa_ref[...], sina_ref[...]
    cb, sb = cosb_ref[...], sinb_ref[...]
    cos = ca * cb - sa * sb
    sin = sa * cb + ca * sb

    def rotary(p, out_ref, post_scale):
        half = RET_DIM // 2
        for h in range(RET_HEADS):
            t1 = p[:, h * RET_DIM:h * RET_DIM + half]
            t2 = p[:, h * RET_DIM + half:(h + 1) * RET_DIM]
            o1 = t1 * cos - t2 * sin
            o2 = t1 * sin + t2 * cos
            if post_scale is not None:
                o1 = o1 * post_scale
                o2 = o2 * post_scale
            out_ref[:, h * RET_DIM:h * RET_DIM + half] = o1.astype(BF16)
            out_ref[:, h * RET_DIM + half:(h + 1) * RET_DIM] = o2.astype(BF16)

    heads(seg(0), qt_ref, qg_ref, (SB_HEAD_DIM ** -0.5) * LOG2E, transpose=True)
    heads(seg(1), k_ref, kg_ref)
    heads(seg(2), vt_ref, transpose=True)
    sg_ref[...] = _silu(seg(3))
    rotary(seg(4), rq_ref, None)
    rotary(seg(5), rk_ref, RET_DIM ** -0.5)
    rg_ref[...] = _silu(seg(7))
    msb_ref[...] = jax.nn.sigmoid(seg(8) + bm_ref[0:1, :])
    mret_ref[...] = jax.nn.sigmoid(seg(9) + bm_ref[1:2, :])
    rv_ref[...] = seg(6).astype(BF16)


def _inproj(x2, norm_gain, w_in_bf, q_gain, k_gain, b_merge, seq):
    m = x2.shape[0]
    tm = IN_TM
    pos_blocks = seq // tm
    cosa, sina, cosb, sinb = _rope_tables(seq, tm)
    row = lambda i: (i, 0)
    const = lambda i: (0, 0)
    tile_base = pl.BlockSpec((None, 1, RET_DIM // 2), lambda i: (i % pos_blocks, 0, 0))
    bf_out = jax.ShapeDtypeStruct((m, D_MODEL), BF16)
    f32_out = jax.ShapeDtypeStruct((m, D_MODEL), F32)
    out_spec = pl.BlockSpec((tm, D_MODEL), row)
    t_out = jax.ShapeDtypeStruct((m // seq, D_MODEL, seq), BF16)
    t_spec = pl.BlockSpec((None, D_MODEL, tm), lambda i: (i // pos_blocks, 0, i % pos_blocks))
    return pl.pallas_call(
        _inproj_kernel,
        grid=(m // tm,),
        in_specs=[
            pl.BlockSpec((tm, D_MODEL), row),
            pl.BlockSpec((1, D_MODEL), const),
            pl.BlockSpec((D_MODEL, N_SEGMENTS * D_MODEL), const),
            pl.BlockSpec((1, SB_HEAD_DIM), const),
            pl.BlockSpec((1, SB_HEAD_DIM), const),
            tile_base, tile_base,
            pl.BlockSpec((tm, RET_DIM // 2), const),
            pl.BlockSpec((tm, RET_DIM // 2), const),
            pl.BlockSpec((2, D_MODEL), const),
        ],
        out_specs=[t_spec, out_spec, t_spec] + [out_spec] * 7,
        out_shape=[t_out, bf_out, t_out, f32_out, bf_out, bf_out, bf_out,
                   f32_out, f32_out, f32_out],
        compiler_params=pltpu.CompilerParams(
            dimension_semantics=("arbitrary",),
            vmem_limit_bytes=VMEM_LIMIT_BYTES),
        name="inproj",
    )(x2, norm_gain, w_in_bf, q_gain, k_gain, cosa, sina, cosb, sinb, b_merge)


def _softplus2(z2):
    return jnp.maximum(z2, 0.0) + jnp.log(1.0 + jnp.exp2(-jnp.abs(z2))) * LOG2E


def _bm_kernel(qt_ref, kc_ref, kp_ref, vtc_ref, vtp_ref, sg_ref,
               rq_ref, rk_ref, rv_ref, rg_ref, dec_ref, qd_ref, kd_ref, gl_ref, og_ref,
               msb_ref, mret_ref, x_ref, wsb_ref, wret_ref, wout_ref, tri_ref,
               k_hbm, vt_hbm,
               o_ref,
               oa_ref, ob_ref, mg_ref, state_ref, rs_ref, rq_state_ref, z_ref, sp_ref, acc_ref,
               carry_ref, kbuf_ref, vbuf_ref, *, tiles_per_seq):
    m = pl.program_id(0)
    cur = jnp.minimum(m, pl.num_programs(0) - 2)
    b = cur // tiles_per_seq
    t = cur % tiles_per_seq
    has_prev = t > 0
    slot = m % 2
    tri = tri_ref[...]

    @pl.when(m == 0)
    def _():
        def zero_rows(r, _):
            rows = pl.ds(pl.multiple_of(r * 16, 16), 16)
            oa_ref[1, rows, :] = jnp.zeros((16, D_MODEL), BF16)
            ob_ref[1, rows, :] = jnp.zeros((16, D_MODEL), BF16)
            return 0

        lax.fori_loop(0, TM // 16, zero_rows, 0)

    def head_cols(h):
        return slice(h * SB_HEAD_DIM, (h + 1) * SB_HEAD_DIM)

    def causal_mask():
        row = lax.broadcasted_iota(jnp.int32, (TM, TM), 0)
        col = lax.broadcasted_iota(jnp.int32, (TM, TM), 1)
        return row < col

    def write_head(h):
        oa_ref[slot, :, head_cols(h)] = (
            acc_ref[h].T * sg_ref[:, head_cols(h)]).astype(BF16)

    n_piece = D_MODEL // TM

    def merged_piece(j):
        cols = slice(j * TM, (j + 1) * TM)
        pa = jnp.dot(oa_ref[1 - slot], wsb_ref[:, cols], preferred_element_type=F32)
        pb = jnp.dot(ob_ref[1 - slot], wret_ref[:, cols], preferred_element_type=F32)
        mg_ref[:, cols] = (msb_ref[:, cols] * pa + mret_ref[:, cols] * pb).astype(BF16)

    def out_piece(j):
        cols = slice(j * TM, (j + 1) * TM)
        o_ref[:, cols] = x_ref[:, cols] + jnp.dot(mg_ref[...], wout_ref[:, cols],
                                                  preferred_element_type=F32)

    def ret_cols(h):
        return slice(h * RET_DIM, (h + 1) * RET_DIM)

    def retention_scores(h):
        q = rq_ref[:, ret_cols(h)]
        k = rk_ref[:, ret_cols(h)]
        v = rv_ref[:, ret_cols(h)]
        rs_ref[h] = lax.dot_general(q, k, (((1,), (1,)), ((), ())),
                                    preferred_element_type=F32)
        state = jnp.where(has_prev, state_ref[h], 0.0)
        rq_state_ref[h] = jnp.dot(q, state.astype(BF16), preferred_element_type=F32)
        kdt = (k.astype(F32) * kd_ref[h]).T.astype(BF16)
        state_ref[h] = state * gl_ref[h] + jnp.dot(kdt, v, preferred_element_type=F32)

    def retention_out(h):
        p = (rs_ref[h] * dec_ref[h]).astype(BF16)
        o = jnp.dot(p, rv_ref[:, ret_cols(h)], preferred_element_type=F32)
        o = o + qd_ref[h] * rq_state_ref[h]
        ob_ref[slot, :, ret_cols(h)] = (
            _rms(o, og_ref[h]) * rg_ref[:, ret_cols(h)]).astype(BF16)

    units = [(h, d) for d in (0, 1) for h in range(SB_HEADS)]
    n_units = len(units)
    for u, (h, d) in enumerate(units):
        kt = (kc_ref if d == 0 else kp_ref)[:, head_cols(h)]
        z_ref[u] = jnp.dot(kt, qt_ref[head_cols(h), :], preferred_element_type=F32)
        if (u + 1) % (n_units // RET_HEADS) == 0:
            retention_scores(u // (n_units // RET_HEADS))
    for u, (h, d) in enumerate(units):
        if u % (n_units // n_piece) == 0:
            merged_piece(u // (n_units // n_piece))
        sp2 = _softplus2(z_ref[u])
        sp2 = jnp.where(causal_mask() if d == 0 else has_prev, sp2, 0.0)
        sp_ref[u] = sp2.astype(BF16)
    total = {}

    def cumsum(h):
        for u in (h, SB_HEADS + h):
            cs = jnp.dot(tri, sp_ref[u], preferred_element_type=F32)
            z_ref[u] = z_ref[u] - cs
            total[u] = cs[0:1, :]

    ahead = 4
    for h in range(ahead):
        cumsum(h)
    for h in range(SB_HEADS):
        if h + ahead < SB_HEADS:
            cumsum(h + ahead)
        if h % (SB_HEADS // RET_HEADS) == 0:
            retention_out(h // (SB_HEADS // RET_HEADS))
        w = jnp.where(causal_mask(), jnp.exp2(z_ref[h]), 0.0)
        carry = total[h]
        pv = jnp.dot(vtc_ref[head_cols(h), :], w.astype(BF16), preferred_element_type=F32)
        w = jnp.where(has_prev, jnp.exp2(z_ref[SB_HEADS + h] - carry), 0.0)
        pv = pv + jnp.dot(vtp_ref[head_cols(h), :], w.astype(BF16),
                          preferred_element_type=F32)
        acc_ref[h] = pv
        carry_ref[h] = carry + total[SB_HEADS + h]
        write_head(h)
        if (h + 1) % (SB_HEADS // n_piece) == 0:
            out_piece(h // (SB_HEADS // n_piece))

    def unfinished(h):
        return jnp.min(carry_ref[h]) < SB_DONE_LOG2

    @pl.when(jnp.logical_and(t >= 2, jnp.min(carry_ref[...]) < SB_DONE_LOG2))
    def _():
        for h in range(SB_HEADS):
            def more(state):
                d, go = state
                return jnp.logical_and(d <= t, go)

            def sweep(state, h=h):
                d, _ = state
                tile = t - d
                row0 = pl.multiple_of((b * tiles_per_seq + tile) * TM, TM)
                col0 = pl.multiple_of(tile * TM, TM)
                pltpu.sync_copy(k_hbm.at[pl.ds(row0, TM), head_cols(h)], kbuf_ref)
                pltpu.sync_copy(vt_hbm.at[b, head_cols(h), pl.ds(col0, TM)], vbuf_ref)
                z2 = jnp.dot(kbuf_ref[...], qt_ref[head_cols(h), :],
                             preferred_element_type=F32)
                cs = jnp.dot(tri, _softplus2(z2).astype(BF16), preferred_element_type=F32)
                carry = carry_ref[h]
                w = jnp.exp2(z2 - cs - carry)
                carry_ref[h] = carry + cs[0:1, :]
                acc_ref[h] += jnp.dot(vbuf_ref[...], w.astype(BF16),
                                      preferred_element_type=F32)
                return d + 1, unfinished(h)

            lax.while_loop(more, sweep, (jnp.int32(2), unfinished(h)))
            write_head(h)


def _retention_tables():
    log_gamma = np.log1p(-np.exp2(-5.0 - np.arange(RET_HEADS)))
    t = np.arange(TM, dtype=np.float64)
    ct = np.arange(TM) // CHUNK
    dist = t[:, None] - t[None, :]
    lg = log_gamma[:, None, None]
    same = (ct[:, None] == ct[None, :])[None]
    earlier = (ct[None, :] < ct[:, None])[None]
    decay = np.where(same, np.exp(lg * np.abs(dist)[None]),
                     np.where(earlier, np.exp(lg * dist[None]), 0.0))
    ones = np.ones((1, 1, RET_DIM))
    qdec = np.exp(log_gamma[:, None] * (t + 1.0)[None, :])[:, :, None] * ones
    kdec = np.exp(log_gamma[:, None] * (TM - 1.0 - t)[None, :])[:, :, None] * ones
    gl = np.exp(log_gamma * TM)[:, None, None] * ones
    return tuple(jnp.asarray(a, F32) for a in (decay, qdec, kdec, gl))


def _branches_merge(qt, k, vt, sg, rq, rk, rv, rg, msb, mret, x2, ret_out_gain,
                    wsb, wret, wout, batch, seq):
    m = x2.shape[0]
    n_tiles = m // TM
    tiles_per_seq = seq // TM
    last = n_tiles - 1
    tri = (jnp.arange(TM)[None, :] >= jnp.arange(TM)[:, None]).astype(BF16)
    decay, qdec, kdec, gl = _retention_tables()
    gain = ret_out_gain.reshape(RET_HEADS, 1, RET_DIM)

    cur = lambda i: jnp.minimum(i, last)
    row_cur = lambda i: (cur(i), 0)
    row_prev_tile = lambda i: (jnp.maximum(cur(i) - 1, 0), 0)
    row_lag = lambda i: (jnp.maximum(i - 1, 0), 0)
    t_cur = lambda i: (cur(i) // tiles_per_seq, 0, cur(i) % tiles_per_seq)
    t_prev = lambda i: (cur(i) // tiles_per_seq, 0,
                        jnp.maximum(cur(i) % tiles_per_seq - 1, 0))
    full2 = lambda i: (0, 0)
    full3 = lambda i: (0, 0, 0)

    rows = lambda imap: pl.BlockSpec((TM, D_MODEL), imap)
    trans = lambda imap: pl.BlockSpec((None, D_MODEL, TM), imap)
    table = lambda a: pl.BlockSpec(a.shape, full3)
    weight = pl.BlockSpec((D_MODEL, D_MODEL), full2)
    hbm = pl.BlockSpec(memory_space=pl.ANY)
    return pl.pallas_call(
        functools.partial(_bm_kernel, tiles_per_seq=tiles_per_seq),
        grid=(n_tiles + 1,),
        in_specs=[
            trans(t_cur), rows(row_cur), rows(row_prev_tile), trans(t_cur), trans(t_prev),
            rows(row_cur),
            rows(row_cur), rows(row_cur), rows(row_cur), rows(row_cur),
            table(decay), table(qdec), table(kdec), table(gl), table(gain),
            rows(row_lag), rows(row_lag), rows(row_lag), weight, weight, weight,
            pl.BlockSpec((TM, TM), full2),
            hbm, hbm,
        ],
        out_specs=rows(row_lag),
        out_shape=jax.ShapeDtypeStruct((m, D_MODEL), F32),
        scratch_shapes=[
            pltpu.VMEM((2, TM, D_MODEL), BF16),
            pltpu.VMEM((2, TM, D_MODEL), BF16),
            pltpu.VMEM((TM, D_MODEL), BF16),
            pltpu.VMEM((RET_HEADS, RET_DIM, RET_DIM), F32),
            pltpu.VMEM((RET_HEADS, TM, TM), F32),
            pltpu.VMEM((RET_HEADS, TM, RET_DIM), F32),
            pltpu.VMEM((2 * SB_HEADS, TM, TM), F32),
            pltpu.VMEM((2 * SB_HEADS, TM, TM), BF16),
            pltpu.VMEM((SB_HEADS, SB_HEAD_DIM, TM), F32),
            pltpu.VMEM((SB_HEADS, 1, TM), F32),
            pltpu.VMEM((TM, SB_HEAD_DIM), BF16),
            pltpu.VMEM((SB_HEAD_DIM, TM), BF16),
        ],
        compiler_params=pltpu.CompilerParams(
            dimension_semantics=("arbitrary",),
            vmem_limit_bytes=VMEM_LIMIT_BYTES),
        name="branches_merge",
    )(qt, k, k, vt, vt, sg, rq, rk, rv, rg, decay, qdec, kdec, gl, gain,
      msb, mret, x2, wsb, wret, wout, tri, k, vt)


def _rope_tables(seq, tile):
    d = RET_DIM
    inv_freq = ROPE_BASE ** (-np.arange(0, d, 2, dtype=np.float64) / d)
    base = (np.arange(seq // tile, dtype=np.float64) * tile)[:, None, None] * inv_freq
    local = np.arange(tile, dtype=np.float64)[:, None] * inv_freq
    return tuple(jnp.asarray(a, F32)
                 for a in (np.cos(base), np.sin(base), np.cos(local), np.sin(local)))


def kernel(x, norm_gain, w_in, b_merge, sb_q_gain, sb_k_gain, ret_out_gain,
           w_branch_sb, w_branch_ret, w_out):
    batch, seq, d_model = x.shape
    depth = norm_gain.shape[0]
    assert d_model == D_MODEL and w_in.shape[-1] == N_SEGMENTS * D_MODEL
    assert seq % TM == 0 and seq % IN_TM == 0 and TM % CHUNK == 0
    x2 = x.reshape(batch * seq, D_MODEL)
    for layer in range(depth):
        qt, k, vt, sg, rq, rk, rv, rg, msb, mret = _inproj(
            x2, norm_gain[layer][None, :], w_in[layer].astype(BF16),
            sb_q_gain[layer][None, :], sb_k_gain[layer][None, :], b_merge[layer], seq)
        x2 = _branches_merge(
            qt, k, vt, sg, rq, rk, rv, rg, msb, mret, x2, ret_out_gain[layer],
            w_branch_sb[layer].astype(BF16), w_branch_ret[layer].astype(BF16),
            w_out[layer].astype(BF16), batch, seq)
    return x2.reshape(batch, seq, D_MODEL)
```

```python
import functools

import numpy as np
import jax
import jax.numpy as jnp
from jax import lax
from jax.experimental import pallas as pl
from jax.experimental.pallas import tpu as pltpu

D_MODEL = 1024
SB_HEADS = 8
SB_HEAD_DIM = D_MODEL // SB_HEADS
RET_HEADS = 4
RET_DIM = D_MODEL // RET_HEADS
CHUNK = 64
ROPE_BASE = 10000.0
EPS = 1e-6
N_SEGMENTS = 10
LOG2E = 1.4426950408889634

VMEM_LIMIT_BYTES = 56 * 1024 * 1024

TM = 256
SB_DONE_LOG2 = 160.0

F32 = jnp.float32
BF16 = jnp.bfloat16


def _rms(x, gain):
    y = x * lax.rsqrt(jnp.mean(x * x, axis=-1, keepdims=True) + EPS)
    return y * gain


def _silu(g):
    return g * jax.nn.sigmoid(g)


def _inproj_kernel(x_ref, ng_ref, w_ref, qg_ref, kg_ref, cosa_ref, sina_ref, cosb_ref,
                   sinb_ref, bm_ref,
                   qt_ref, k_ref, vt_ref, sg_ref, rq_ref, rk_ref, rv_ref, rg_ref,
                   msb_ref, mret_ref):
    hb = _rms(x_ref[...], ng_ref[...]).astype(BF16)

    def seg(s):
        return jnp.dot(hb, w_ref[:, s * D_MODEL:(s + 1) * D_MODEL],
                       preferred_element_type=F32)

    def heads(p, out_ref, gain_ref=None, post_scale=None, transpose=False):
        for h in range(SB_HEADS):
            sl = slice(h * SB_HEAD_DIM, (h + 1) * SB_HEAD_DIM)
            y = p[:, sl]
            if gain_ref is not None:
                y = _rms(y, gain_ref[...])
            if post_scale is not None:
                y = y * post_scale
            if transpose:
                out_ref[sl, :] = y.T.astype(BF16)
            else:
                out_ref[:, sl] = y.astype(BF16)

    ca, sa = cosa_ref[...], sina_ref[...]
    cb, sb = cosb_ref[...], sinb_ref[...]
    cos = ca * cb - sa * sb
    sin = sa * cb + ca * sb

    def rotary(p, out_ref, post_scale):
        half = RET_DIM // 2
        for h in range(RET_HEADS):
            t1 = p[:, h * RET_DIM:h * RET_DIM + half]
            t2 = p[:, h * RET_DIM + half:(h + 1) * RET_DIM]
            o1 = t1 * cos - t2 * sin
            o2 = t1 * sin + t2 * cos
            if post_scale is not None:
                o1 = o1 * post_scale
                o2 = o2 * post_scale
            out_ref[:, h * RET_DIM:h * RET_DIM + half] = o1.astype(BF16)
            out_ref[:, h * RET_DIM + half:(h + 1) * RET_DIM] = o2.astype(BF16)

    heads(seg(0), qt_ref, qg_ref, (SB_HEAD_DIM ** -0.5) * LOG2E, transpose=True)
    heads(seg(1), k_ref, kg_ref)
    heads(seg(2), vt_ref, transpose=True)
    sg_ref[...] = _silu(seg(3))
    rotary(seg(4), rq_ref, None)
    rotary(seg(5), rk_ref, RET_DIM ** -0.5)
    rg_ref[...] = _silu(seg(7))
    msb_ref[...] = jax.nn.sigmoid(seg(8) + bm_ref[0:1, :])
    mret_ref[...] = jax.nn.sigmoid(seg(9) + bm_ref[1:2, :])
    rv_ref[...] = seg(6).astype(BF16)


def _inproj(x2, norm_gain, w_in_bf, q_gain, k_gain, b_merge, seq):
    m = x2.shape[0]
    tm = TM
    pos_blocks = seq // tm
    cosa, sina, cosb, sinb = _rope_tables(seq, tm)
    row = lambda i: (i, 0)
    const = lambda i: (0, 0)
    tile_base = pl.BlockSpec((None, 1, RET_DIM // 2), lambda i: (i % pos_blocks, 0, 0))
    bf_out = jax.ShapeDtypeStruct((m, D_MODEL), BF16)
    f32_out = jax.ShapeDtypeStruct((m, D_MODEL), F32)
    out_spec = pl.BlockSpec((tm, D_MODEL), row)
    t_out = jax.ShapeDtypeStruct((m // tm, D_MODEL, tm), BF16)
    t_spec = pl.BlockSpec((None, D_MODEL, tm), lambda i: (i, 0, 0))
    return pl.pallas_call(
        _inproj_kernel,
        grid=(m // tm,),
        in_specs=[
            pl.BlockSpec((tm, D_MODEL), row),
            pl.BlockSpec((1, D_MODEL), const),
            pl.BlockSpec((D_MODEL, N_SEGMENTS * D_MODEL), const),
            pl.BlockSpec((1, SB_HEAD_DIM), const),
            pl.BlockSpec((1, SB_HEAD_DIM), const),
            tile_base, tile_base,
            pl.BlockSpec((tm, RET_DIM // 2), const),
            pl.BlockSpec((tm, RET_DIM // 2), const),
            pl.BlockSpec((2, D_MODEL), const),
        ],
        out_specs=[t_spec, out_spec, t_spec] + [out_spec] * 7,
        out_shape=[t_out, bf_out, t_out, f32_out, bf_out, bf_out, bf_out,
                   f32_out, f32_out, f32_out],
        compiler_params=pltpu.CompilerParams(
            dimension_semantics=("arbitrary",),
            vmem_limit_bytes=VMEM_LIMIT_BYTES),
        name="inproj",
    )(x2, norm_gain, w_in_bf, q_gain, k_gain, cosa, sina, cosb, sinb, b_merge)


def _softplus2(z2):
    return jnp.maximum(z2, 0.0) + jnp.log(1.0 + jnp.exp2(-jnp.abs(z2))) * LOG2E


def _bm_kernel(qt_ref, kc_ref, kp_ref, vtc_ref, vtp_ref, sg_ref,
               rq_ref, rk_ref, rv_ref, rg_ref, dec_ref, qd_ref, kd_ref, gl_ref, og_ref,
               msb_ref, mret_ref, x_ref, wsb_ref, wret_ref, wout_ref, tri_ref,
               k_hbm, vt_hbm,
               o_ref,
               oa_ref, ob_ref, mg_ref, state_ref, rs_ref, rq_state_ref, z_ref, sp_ref, acc_ref,
               carry_ref, kbuf_ref, vbuf_ref, *, tiles_per_seq):
    m = pl.program_id(0)
    cur = jnp.minimum(m, pl.num_programs(0) - 2)
    t = cur % tiles_per_seq
    has_prev = t > 0
    slot = m % 2
    tri = tri_ref[...]

    @pl.when(m == 0)
    def _():
        def zero_rows(r, _):
            rows = pl.ds(pl.multiple_of(r * 16, 16), 16)
            oa_ref[1, rows, :] = jnp.zeros((16, D_MODEL), BF16)
            ob_ref[1, rows, :] = jnp.zeros((16, D_MODEL), BF16)
            return 0

        lax.fori_loop(0, TM // 16, zero_rows, 0)

    def head_cols(h):
        return slice(h * SB_HEAD_DIM, (h + 1) * SB_HEAD_DIM)

    def causal_mask():
        row = lax.broadcasted_iota(jnp.int32, (TM, TM), 0)
        col = lax.broadcasted_iota(jnp.int32, (TM, TM), 1)
        return row < col

    def write_head(h):
        oa_ref[slot, :, head_cols(h)] = (
            acc_ref[h].T * sg_ref[:, head_cols(h)]).astype(BF16)

    n_piece = D_MODEL // TM

    def merged_piece(j):
        cols = slice(j * TM, (j + 1) * TM)
        pa = jnp.dot(oa_ref[1 - slot], wsb_ref[:, cols], preferred_element_type=F32)
        pb = jnp.dot(ob_ref[1 - slot], wret_ref[:, cols], preferred_element_type=F32)
        mg_ref[:, cols] = (msb_ref[:, cols] * pa + mret_ref[:, cols] * pb).astype(BF16)

    def out_piece(j):
        cols = slice(j * TM, (j + 1) * TM)
        o_ref[:, cols] = x_ref[:, cols] + jnp.dot(mg_ref[...], wout_ref[:, cols],
                                                  preferred_element_type=F32)

    def ret_cols(h):
        return slice(h * RET_DIM, (h + 1) * RET_DIM)

    def retention_scores(h):
        q = rq_ref[:, ret_cols(h)]
        k = rk_ref[:, ret_cols(h)]
        v = rv_ref[:, ret_cols(h)]
        rs_ref[h] = lax.dot_general(q, k, (((1,), (1,)), ((), ())),
                                    preferred_element_type=F32)
        state = jnp.where(has_prev, state_ref[h], 0.0)
        rq_state_ref[h] = jnp.dot(q, state.astype(BF16), preferred_element_type=F32)
        kdt = (k.astype(F32) * kd_ref[h]).T.astype(BF16)
        state_ref[h] = state * gl_ref[h] + jnp.dot(kdt, v, preferred_element_type=F32)

    def retention_out(h):
        p = (rs_ref[h] * dec_ref[h]).astype(BF16)
        o = jnp.dot(p, rv_ref[:, ret_cols(h)], preferred_element_type=F32)
        o = o + qd_ref[h] * rq_state_ref[h]
        ob_ref[slot, :, ret_cols(h)] = (
            _rms(o, og_ref[h]) * rg_ref[:, ret_cols(h)]).astype(BF16)

    units = [(h, d) for d in (0, 1) for h in range(SB_HEADS)]
    n_units = len(units)
    for u, (h, d) in enumerate(units):
        kt = (kc_ref if d == 0 else kp_ref)[:, head_cols(h)]
        z_ref[u] = jnp.dot(kt, qt_ref[head_cols(h), :], preferred_element_type=F32)
        if (u + 1) % (n_units // RET_HEADS) == 0:
            retention_scores(u // (n_units // RET_HEADS))
    for u, (h, d) in enumerate(units):
        if u % (n_units // n_piece) == 0:
            merged_piece(u // (n_units // n_piece))
        sp2 = _softplus2(z_ref[u])
        sp2 = jnp.where(causal_mask() if d == 0 else has_prev, sp2, 0.0)
        sp_ref[u] = sp2.astype(BF16)
    total = {}

    def cumsum(h):
        for u in (h, SB_HEADS + h):
            cs = jnp.dot(tri, sp_ref[u], preferred_element_type=F32)
            z_ref[u] = z_ref[u] - cs
            total[u] = cs[0:1, :]

    ahead = 4
    for h in range(ahead):
        cumsum(h)
    for h in range(SB_HEADS):
        if h + ahead < SB_HEADS:
            cumsum(h + ahead)
        if h % (SB_HEADS // RET_HEADS) == 0:
            retention_out(h // (SB_HEADS // RET_HEADS))
        w = jnp.where(causal_mask(), jnp.exp2(z_ref[h]), 0.0)
        carry = total[h]
        pv = jnp.dot(vtc_ref[head_cols(h), :], w.astype(BF16), preferred_element_type=F32)
        w = jnp.where(has_prev, jnp.exp2(z_ref[SB_HEADS + h] - carry), 0.0)
        pv = pv + jnp.dot(vtp_ref[head_cols(h), :], w.astype(BF16),
                          preferred_element_type=F32)
        acc_ref[h] = pv
        carry_ref[h] = carry + total[SB_HEADS + h]
        write_head(h)
        if (h + 1) % (SB_HEADS // n_piece) == 0:
            out_piece(h // (SB_HEADS // n_piece))

    def unfinished(h):
        return jnp.min(carry_ref[h]) < SB_DONE_LOG2

    @pl.when(jnp.logical_and(t >= 2, jnp.min(carry_ref[...]) < SB_DONE_LOG2))
    def _():
        for h in range(SB_HEADS):
            def more(state):
                d, go = state
                return jnp.logical_and(d <= t, go)

            def sweep(state, h=h):
                d, _ = state
                tile = cur - d
                row0 = pl.multiple_of(tile * TM, TM)
                pltpu.sync_copy(k_hbm.at[pl.ds(row0, TM), head_cols(h)], kbuf_ref)
                pltpu.sync_copy(vt_hbm.at[tile, head_cols(h), :], vbuf_ref)
                z2 = jnp.dot(kbuf_ref[...], qt_ref[head_cols(h), :],
                             preferred_element_type=F32)
                cs = jnp.dot(tri, _softplus2(z2).astype(BF16), preferred_element_type=F32)
                carry = carry_ref[h]
                w = jnp.exp2(z2 - cs - carry)
                carry_ref[h] = carry + cs[0:1, :]
                acc_ref[h] += jnp.dot(vbuf_ref[...], w.astype(BF16),
                                      preferred_element_type=F32)
                return d + 1, unfinished(h)

            lax.while_loop(more, sweep, (jnp.int32(2), unfinished(h)))
            write_head(h)


def _retention_tables():
    log_gamma = np.log1p(-np.exp2(-5.0 - np.arange(RET_HEADS)))
    t = np.arange(TM, dtype=np.float64)
    ct = np.arange(TM) // CHUNK
    dist = t[:, None] - t[None, :]
    lg = log_gamma[:, None, None]
    same = (ct[:, None] == ct[None, :])[None]
    earlier = (ct[None, :] < ct[:, None])[None]
    decay = np.where(same, np.exp(lg * np.abs(dist)[None]),
                     np.where(earlier, np.exp(lg * dist[None]), 0.0))
    ones = np.ones((1, 1, RET_DIM))
    qdec = np.exp(log_gamma[:, None] * (t + 1.0)[None, :])[:, :, None] * ones
    kdec = np.exp(log_gamma[:, None] * (TM - 1.0 - t)[None, :])[:, :, None] * ones
    gl = np.exp(log_gamma * TM)[:, None, None] * ones
    return tuple(jnp.asarray(a, F32) for a in (decay, qdec, kdec, gl))


def _branches_merge(qt, k, vt, sg, rq, rk, rv, rg, msb, mret, x2, ret_out_gain,
                    wsb, wret, wout, batch, seq):
    m = x2.shape[0]
    n_tiles = m // TM
    tiles_per_seq = seq // TM
    last = n_tiles - 1
    tri = (jnp.arange(TM)[None, :] >= jnp.arange(TM)[:, None]).astype(BF16)
    decay, qdec, kdec, gl = _retention_tables()
    gain = ret_out_gain.reshape(RET_HEADS, 1, RET_DIM)

    cur = lambda i: jnp.minimum(i, last)
    row_cur = lambda i: (cur(i), 0)
    row_prev_tile = lambda i: (jnp.maximum(cur(i) - 1, 0), 0)
    row_lag = lambda i: (jnp.maximum(i - 1, 0), 0)
    t_cur = lambda i: (cur(i), 0, 0)
    t_prev = lambda i: (jnp.maximum(cur(i) - 1, 0), 0, 0)
    full2 = lambda i: (0, 0)
    full3 = lambda i: (0, 0, 0)

    rows = lambda imap: pl.BlockSpec((TM, D_MODEL), imap)
    trans = lambda imap: pl.BlockSpec((None, D_MODEL, TM), imap)
    table = lambda a: pl.BlockSpec(a.shape, full3)
    weight = pl.BlockSpec((D_MODEL, D_MODEL), full2)
    hbm = pl.BlockSpec(memory_space=pl.ANY)
    return pl.pallas_call(
        functools.partial(_bm_kernel, tiles_per_seq=tiles_per_seq),
        grid=(n_tiles + 1,),
        in_specs=[
            trans(t_cur), rows(row_cur), rows(row_prev_tile), trans(t_cur), trans(t_prev),
            rows(row_cur),
            rows(row_cur), rows(row_cur), rows(row_cur), rows(row_cur),
            table(decay), table(qdec), table(kdec), table(gl), table(gain),
            rows(row_lag), rows(row_lag), rows(row_lag), weight, weight, weight,
            pl.BlockSpec((TM, TM), full2),
            hbm, hbm,
        ],
        out_specs=rows(row_lag),
        out_shape=jax.ShapeDtypeStruct((m, D_MODEL), F32),
        scratch_shapes=[
            pltpu.VMEM((2, TM, D_MODEL), BF16),
            pltpu.VMEM((2, TM, D_MODEL), BF16),
            pltpu.VMEM((TM, D_MODEL), BF16),
            pltpu.VMEM((RET_HEADS, RET_DIM, RET_DIM), F32),
            pltpu.VMEM((RET_HEADS, TM, TM), F32),
            pltpu.VMEM((RET_HEADS, TM, RET_DIM), F32),
            pltpu.VMEM((2 * SB_HEADS, TM, TM), F32),
            pltpu.VMEM((2 * SB_HEADS, TM, TM), BF16),
            pltpu.VMEM((SB_HEADS, SB_HEAD_DIM, TM), F32),
            pltpu.VMEM((SB_HEADS, 1, TM), F32),
            pltpu.VMEM((TM, SB_HEAD_DIM), BF16),
            pltpu.VMEM((SB_HEAD_DIM, TM), BF16),
        ],
        compiler_params=pltpu.CompilerParams(
            dimension_semantics=("arbitrary",),
            vmem_limit_bytes=VMEM_LIMIT_BYTES),
        name="branches_merge",
    )(qt, k, k, vt, vt, sg, rq, rk, rv, rg, decay, qdec, kdec, gl, gain,
      msb, mret, x2, wsb, wret, wout, tri, k, vt)


def _rope_tables(seq, tile):
    d = RET_DIM
    inv_freq = ROPE_BASE ** (-np.arange(0, d, 2, dtype=np.float64) / d)
    base = (np.arange(seq // tile, dtype=np.float64) * tile)[:, None, None] * inv_freq
    local = np.arange(tile, dtype=np.float64)[:, None] * inv_freq
    return tuple(jnp.asarray(a, F32)
                 for a in (np.cos(base), np.sin(base), np.cos(local), np.sin(local)))


def kernel(x, norm_gain, w_in, b_merge, sb_q_gain, sb_k_gain, ret_out_gain,
           w_branch_sb, w_branch_ret, w_out):
    batch, seq, d_model = x.shape
    depth = norm_gain.shape[0]
    assert d_model == D_MODEL and w_in.shape[-1] == N_SEGMENTS * D_MODEL
    assert seq % TM == 0 and TM % CHUNK == 0
    x2 = x.reshape(batch * seq, D_MODEL)
    for layer in range(depth):
        qt, k, vt, sg, rq, rk, rv, rg, msb, mret = _inproj(
            x2, norm_gain[layer][None, :], w_in[layer].astype(BF16),
            sb_q_gain[layer][None, :], sb_k_gain[layer][None, :], b_merge[layer], seq)
        x2 = _branches_merge(
            qt, k, vt, sg, rq, rk, rv, rg, msb, mret, x2, ret_out_gain[layer],
            w_branch_sb[layer].astype(BF16), w_branch_ret[layer].astype(BF16),
            w_out[layer].astype(BF16), batch, seq)
    return x2.reshape(batch, seq, D_MODEL)
```

```python
import functools

import numpy as np
import jax
import jax.numpy as jnp
from jax import lax
from jax.experimental import pallas as pl
from jax.experimental.pallas import tpu as pltpu

D_MODEL = 1024
SB_HEADS = 8
SB_HEAD_DIM = D_MODEL // SB_HEADS
RET_HEADS = 4
RET_DIM = D_MODEL // RET_HEADS
CHUNK = 64
ROPE_BASE = 10000.0
EPS = 1e-6
N_SEGMENTS = 10
IN_SEGMENTS = 8
LOG2E = 1.4426950408889634

VMEM_LIMIT_BYTES = 56 * 1024 * 1024

TM = 256
SB_DONE_LOG2 = 160.0

F32 = jnp.float32
BF16 = jnp.bfloat16


def _rms(x, gain):
    y = x * lax.rsqrt(jnp.mean(x * x, axis=-1, keepdims=True) + EPS)
    return y * gain


def _silu(g):
    return g * jax.nn.sigmoid(g)


def _inproj_kernel(x_ref, ng_ref, w_ref, qg_ref, kg_ref, cosa_ref, sina_ref, cosb_ref,
                   sinb_ref,
                   qt_ref, k_ref, vt_ref, sg_ref, rq_ref, rk_ref, rv_ref, rg_ref):
    hb = _rms(x_ref[...], ng_ref[...]).astype(BF16)

    def seg(s):
        return jnp.dot(hb, w_ref[:, s * D_MODEL:(s + 1) * D_MODEL],
                       preferred_element_type=F32)

    def heads(p, out_ref, gain_ref=None, post_scale=None, transpose=False):
        for h in range(SB_HEADS):
            sl = slice(h * SB_HEAD_DIM, (h + 1) * SB_HEAD_DIM)
            y = p[:, sl]
            if gain_ref is not None:
                y = _rms(y, gain_ref[...])
            if post_scale is not None:
                y = y * post_scale
            if transpose:
                out_ref[sl, :] = y.T.astype(BF16)
            else:
                out_ref[:, sl] = y.astype(BF16)

    ca, sa = cosa_ref[...], sina_ref[...]
    cb, sb = cosb_ref[...], sinb_ref[...]
    cos = ca * cb - sa * sb
    sin = sa * cb + ca * sb

    def rotary(p, out_ref, post_scale):
        half = RET_DIM // 2
        for h in range(RET_HEADS):
            t1 = p[:, h * RET_DIM:h * RET_DIM + half]
            t2 = p[:, h * RET_DIM + half:(h + 1) * RET_DIM]
            o1 = t1 * cos - t2 * sin
            o2 = t1 * sin + t2 * cos
            if post_scale is not None:
                o1 = o1 * post_scale
                o2 = o2 * post_scale
            out_ref[:, h * RET_DIM:h * RET_DIM + half] = o1.astype(BF16)
            out_ref[:, h * RET_DIM + half:(h + 1) * RET_DIM] = o2.astype(BF16)

    heads(seg(0), qt_ref, qg_ref, (SB_HEAD_DIM ** -0.5) * LOG2E, transpose=True)
    heads(seg(1), k_ref, kg_ref)
    heads(seg(2), vt_ref, transpose=True)
    sg_ref[...] = _silu(seg(3))
    rotary(seg(4), rq_ref, None)
    rotary(seg(5), rk_ref, RET_DIM ** -0.5)
    rg_ref[...] = _silu(seg(7))
    rv_ref[...] = seg(6).astype(BF16)


def _inproj(x2, norm_gain, w_in_bf, q_gain, k_gain, seq):
    m = x2.shape[0]
    tm = TM
    pos_blocks = seq // tm
    cosa, sina, cosb, sinb = _rope_tables(seq, tm)
    row = lambda i: (i, 0)
    const = lambda i: (0, 0)
    tile_base = pl.BlockSpec((None, 1, RET_DIM // 2), lambda i: (i % pos_blocks, 0, 0))
    bf_out = jax.ShapeDtypeStruct((m, D_MODEL), BF16)
    f32_out = jax.ShapeDtypeStruct((m, D_MODEL), F32)
    out_spec = pl.BlockSpec((tm, D_MODEL), row)
    t_out = jax.ShapeDtypeStruct((m // tm, D_MODEL, tm), BF16)
    t_spec = pl.BlockSpec((None, D_MODEL, tm), lambda i: (i, 0, 0))
    return pl.pallas_call(
        _inproj_kernel,
        grid=(m // tm,),
        in_specs=[
            pl.BlockSpec((tm, D_MODEL), row),
            pl.BlockSpec((1, D_MODEL), const),
            pl.BlockSpec((D_MODEL, IN_SEGMENTS * D_MODEL), const),
            pl.BlockSpec((1, SB_HEAD_DIM), const),
            pl.BlockSpec((1, SB_HEAD_DIM), const),
            tile_base, tile_base,
            pl.BlockSpec((tm, RET_DIM // 2), const),
            pl.BlockSpec((tm, RET_DIM // 2), const),
        ],
        out_specs=[t_spec, out_spec, t_spec] + [out_spec] * 5,
        out_shape=[t_out, bf_out, t_out, f32_out, bf_out, bf_out, bf_out, f32_out],
        compiler_params=pltpu.CompilerParams(
            dimension_semantics=("arbitrary",),
            vmem_limit_bytes=VMEM_LIMIT_BYTES),
        name="inproj",
    )(x2, norm_gain, w_in_bf, q_gain, k_gain, cosa, sina, cosb, sinb)


def _softplus2(z2):
    return jnp.maximum(z2, 0.0) + jnp.log(1.0 + jnp.exp2(-jnp.abs(z2))) * LOG2E


def _bm_kernel(qt_ref, kc_ref, kp_ref, vtc_ref, vtp_ref, sg_ref,
               rq_ref, rk_ref, rv_ref, rg_ref, dec_ref, qd_ref, kd_ref, gl_ref, og_ref,
               xc_ref, ng_ref, wm_ref, bm_ref, x_ref, wsb_ref, wret_ref, wout_ref, tri_ref,
               k_hbm, vt_hbm,
               o_ref,
               oa_ref, ob_ref, gate_ref, hb_ref, mg_ref, state_ref, rs_ref, rq_state_ref,
               z_ref, sp_ref, acc_ref, carry_ref, kbuf_ref, vbuf_ref, *, tiles_per_seq):
    m = pl.program_id(0)
    cur = jnp.minimum(m, pl.num_programs(0) - 2)
    t = cur % tiles_per_seq
    has_prev = t > 0
    slot = m % 2
    tri = tri_ref[...]

    @pl.when(m == 0)
    def _():
        def zero_rows(r, _):
            rows = pl.ds(pl.multiple_of(r * 16, 16), 16)
            oa_ref[1, rows, :] = jnp.zeros((16, D_MODEL), BF16)
            ob_ref[1, rows, :] = jnp.zeros((16, D_MODEL), BF16)
            gate_ref[1, rows, :] = jnp.zeros((16, 2 * D_MODEL), F32)
            return 0

        lax.fori_loop(0, TM // 16, zero_rows, 0)

    def head_cols(h):
        return slice(h * SB_HEAD_DIM, (h + 1) * SB_HEAD_DIM)

    def causal_mask():
        row = lax.broadcasted_iota(jnp.int32, (TM, TM), 0)
        col = lax.broadcasted_iota(jnp.int32, (TM, TM), 1)
        return row < col

    def write_head(h):
        oa_ref[slot, :, head_cols(h)] = (
            acc_ref[h].T * sg_ref[:, head_cols(h)]).astype(BF16)

    n_piece = D_MODEL // TM

    def gate_piece(j):
        cols = slice(j * TM, (j + 1) * TM)
        g = jnp.dot(hb_ref[...], wm_ref[:, cols], preferred_element_type=F32)
        gate_ref[slot, :, cols] = jax.nn.sigmoid(g + bm_ref[:, cols])

    def merged_piece(j):
        cols = slice(j * TM, (j + 1) * TM)
        gcols = slice(D_MODEL + j * TM, D_MODEL + (j + 1) * TM)
        pa = jnp.dot(oa_ref[1 - slot], wsb_ref[:, cols], preferred_element_type=F32)
        pb = jnp.dot(ob_ref[1 - slot], wret_ref[:, cols], preferred_element_type=F32)
        mg_ref[:, cols] = (gate_ref[1 - slot, :, cols] * pa
                           + gate_ref[1 - slot, :, gcols] * pb).astype(BF16)

    def out_piece(j):
        cols = slice(j * TM, (j + 1) * TM)
        o_ref[:, cols] = x_ref[:, cols] + jnp.dot(mg_ref[...], wout_ref[:, cols],
                                                  preferred_element_type=F32)

    def ret_cols(h):
        return slice(h * RET_DIM, (h + 1) * RET_DIM)

    def retention_scores(h):
        q = rq_ref[:, ret_cols(h)]
        k = rk_ref[:, ret_cols(h)]
        v = rv_ref[:, ret_cols(h)]
        rs_ref[h] = lax.dot_general(q, k, (((1,), (1,)), ((), ())),
                                    preferred_element_type=F32)
        state = jnp.where(has_prev, state_ref[h], 0.0)
        rq_state_ref[h] = jnp.dot(q, state.astype(BF16), preferred_element_type=F32)
        kdt = (k.astype(F32) * kd_ref[h]).T.astype(BF16)
        state_ref[h] = state * gl_ref[h] + jnp.dot(kdt, v, preferred_element_type=F32)

    def retention_out(h):
        p = (rs_ref[h] * dec_ref[h]).astype(BF16)
        o = jnp.dot(p, rv_ref[:, ret_cols(h)], preferred_element_type=F32)
        o = o + qd_ref[h] * rq_state_ref[h]
        ob_ref[slot, :, ret_cols(h)] = (
            _rms(o, og_ref[h]) * rg_ref[:, ret_cols(h)]).astype(BF16)

    units = [(h, d) for d in (0, 1) for h in range(SB_HEADS)]
    n_units = len(units)
    n_gate = 2 * n_piece
    hb_ref[...] = _rms(xc_ref[...], ng_ref[...]).astype(BF16)
    for u, (h, d) in enumerate(units):
        kt = (kc_ref if d == 0 else kp_ref)[:, head_cols(h)]
        z_ref[u] = jnp.dot(kt, qt_ref[head_cols(h), :], preferred_element_type=F32)
        if (u + 1) % (n_units // RET_HEADS) == 0:
            retention_scores(u // (n_units // RET_HEADS))
        if (u + 3) % (n_units // (n_gate // 2)) == 0:
            gate_piece(u // (n_units // (n_gate // 2)))
    for u, (h, d) in enumerate(units):
        if u % (n_units // n_piece) == 0:
            merged_piece(u // (n_units // n_piece))
        sp2 = _softplus2(z_ref[u])
        sp2 = jnp.where(causal_mask() if d == 0 else has_prev, sp2, 0.0)
        sp_ref[u] = sp2.astype(BF16)
    total = {}

    def cumsum(h):
        for u in (h, SB_HEADS + h):
            cs = jnp.dot(tri, sp_ref[u], preferred_element_type=F32)
            z_ref[u] = z_ref[u] - cs
            total[u] = cs[0:1, :]

    ahead = 4
    for h in range(ahead):
        cumsum(h)
    for h in range(SB_HEADS):
        if h + ahead < SB_HEADS:
            cumsum(h + ahead)
        if h % (SB_HEADS // RET_HEADS) == 0:
            retention_out(h // (SB_HEADS // RET_HEADS))
        else:
            gate_piece(n_gate // 2 + h // (SB_HEADS // RET_HEADS))
        w = jnp.where(causal_mask(), jnp.exp2(z_ref[h]), 0.0)
        carry = total[h]
        pv = jnp.dot(vtc_ref[head_cols(h), :], w.astype(BF16), preferred_element_type=F32)
        w = jnp.where(has_prev, jnp.exp2(z_ref[SB_HEADS + h] - carry), 0.0)
        pv = pv + jnp.dot(vtp_ref[head_cols(h), :], w.astype(BF16),
                          preferred_element_type=F32)
        acc_ref[h] = pv
        carry_ref[h] = carry + total[SB_HEADS + h]
        write_head(h)
        if (h + 1) % (SB_HEADS // n_piece) == 0:
            out_piece(h // (SB_HEADS // n_piece))

    def unfinished(h):
        return jnp.min(carry_ref[h]) < SB_DONE_LOG2

    @pl.when(jnp.logical_and(t >= 2, jnp.min(carry_ref[...]) < SB_DONE_LOG2))
    def _():
        for h in range(SB_HEADS):
            def more(state):
                d, go = state
                return jnp.logical_and(d <= t, go)

            def sweep(state, h=h):
                d, _ = state
                tile = cur - d
                row0 = pl.multiple_of(tile * TM, TM)
                pltpu.sync_copy(k_hbm.at[pl.ds(row0, TM), head_cols(h)], kbuf_ref)
                pltpu.sync_copy(vt_hbm.at[tile, head_cols(h), :], vbuf_ref)
                z2 = jnp.dot(kbuf_ref[...], qt_ref[head_cols(h), :],
                             preferred_element_type=F32)
                cs = jnp.dot(tri, _softplus2(z2).astype(BF16), preferred_element_type=F32)
                carry = carry_ref[h]
                w = jnp.exp2(z2 - cs - carry)
                carry_ref[h] = carry + cs[0:1, :]
                acc_ref[h] += jnp.dot(vbuf_ref[...], w.astype(BF16),
                                      preferred_element_type=F32)
                return d + 1, unfinished(h)

            lax.while_loop(more, sweep, (jnp.int32(2), unfinished(h)))
            write_head(h)


def _retention_tables():
    log_gamma = np.log1p(-np.exp2(-5.0 - np.arange(RET_HEADS)))
    t = np.arange(TM, dtype=np.float64)
    ct = np.arange(TM) // CHUNK
    dist = t[:, None] - t[None, :]
    lg = log_gamma[:, None, None]
    same = (ct[:, None] == ct[None, :])[None]
    earlier = (ct[None, :] < ct[:, None])[None]
    decay = np.where(same, np.exp(lg * np.abs(dist)[None]),
                     np.where(earlier, np.exp(lg * dist[None]), 0.0))
    ones = np.ones((1, 1, RET_DIM))
    qdec = np.exp(log_gamma[:, None] * (t + 1.0)[None, :])[:, :, None] * ones
    kdec = np.exp(log_gamma[:, None] * (TM - 1.0 - t)[None, :])[:, :, None] * ones
    gl = np.exp(log_gamma * TM)[:, None, None] * ones
    return tuple(jnp.asarray(a, F32) for a in (decay, qdec, kdec, gl))


def _branches_merge(qt, k, vt, sg, rq, rk, rv, rg, x2, norm_gain, w_gate, b_gate,
                    ret_out_gain, wsb, wret, wout, seq):
    m = x2.shape[0]
    n_tiles = m // TM
    tiles_per_seq = seq // TM
    last = n_tiles - 1
    tri = (jnp.arange(TM)[None, :] >= jnp.arange(TM)[:, None]).astype(BF16)
    decay, qdec, kdec, gl = _retention_tables()
    gain = ret_out_gain.reshape(RET_HEADS, 1, RET_DIM)

    cur = lambda i: jnp.minimum(i, last)
    row_cur = lambda i: (cur(i), 0)
    row_prev_tile = lambda i: (jnp.maximum(cur(i) - 1, 0), 0)
    row_lag = lambda i: (jnp.maximum(i - 1, 0), 0)
    t_cur = lambda i: (cur(i), 0, 0)
    t_prev = lambda i: (jnp.maximum(cur(i) - 1, 0), 0, 0)
    full2 = lambda i: (0, 0)
    full3 = lambda i: (0, 0, 0)

    rows = lambda imap: pl.BlockSpec((TM, D_MODEL), imap)
    trans = lambda imap: pl.BlockSpec((None, D_MODEL, TM), imap)
    table = lambda a: pl.BlockSpec(a.shape, full3)
    weight = pl.BlockSpec((D_MODEL, D_MODEL), full2)
    hbm = pl.BlockSpec(memory_space=pl.ANY)
    return pl.pallas_call(
        functools.partial(_bm_kernel, tiles_per_seq=tiles_per_seq),
        grid=(n_tiles + 1,),
        in_specs=[
            trans(t_cur), rows(row_cur), rows(row_prev_tile), trans(t_cur), trans(t_prev),
            rows(row_cur),
            rows(row_cur), rows(row_cur), rows(row_cur), rows(row_cur),
            table(decay), table(qdec), table(kdec), table(gl), table(gain),
            rows(row_cur), pl.BlockSpec((1, D_MODEL), full2),
            pl.BlockSpec((D_MODEL, 2 * D_MODEL), full2), pl.BlockSpec((1, 2 * D_MODEL), full2),
            rows(row_lag), weight, weight, weight,
            pl.BlockSpec((TM, TM), full2),
            hbm, hbm,
        ],
        out_specs=rows(row_lag),
        out_shape=jax.ShapeDtypeStruct((m, D_MODEL), F32),
        scratch_shapes=[
            pltpu.VMEM((2, TM, D_MODEL), BF16),
            pltpu.VMEM((2, TM, D_MODEL), BF16),
            pltpu.VMEM((2, TM, 2 * D_MODEL), F32),
            pltpu.VMEM((TM, D_MODEL), BF16),
            pltpu.VMEM((TM, D_MODEL), BF16),
            pltpu.VMEM((RET_HEADS, RET_DIM, RET_DIM), F32),
            pltpu.VMEM((RET_HEADS, TM, TM), F32),
            pltpu.VMEM((RET_HEADS, TM, RET_DIM), F32),
            pltpu.VMEM((2 * SB_HEADS, TM, TM), F32),
            pltpu.VMEM((2 * SB_HEADS, TM, TM), BF16),
            pltpu.VMEM((SB_HEADS, SB_HEAD_DIM, TM), F32),
            pltpu.VMEM((SB_HEADS, 1, TM), F32),
            pltpu.VMEM((TM, SB_HEAD_DIM), BF16),
            pltpu.VMEM((SB_HEAD_DIM, TM), BF16),
        ],
        compiler_params=pltpu.CompilerParams(
            dimension_semantics=("arbitrary",),
            vmem_limit_bytes=VMEM_LIMIT_BYTES),
        name="branches_merge",
    )(qt, k, k, vt, vt, sg, rq, rk, rv, rg, decay, qdec, kdec, gl, gain,
      x2, norm_gain, w_gate, b_gate, x2, wsb, wret, wout, tri, k, vt)


def _rope_tables(seq, tile):
    d = RET_DIM
    inv_freq = ROPE_BASE ** (-np.arange(0, d, 2, dtype=np.float64) / d)
    base = (np.arange(seq // tile, dtype=np.float64) * tile)[:, None, None] * inv_freq
    local = np.arange(tile, dtype=np.float64)[:, None] * inv_freq
    return tuple(jnp.asarray(a, F32)
                 for a in (np.cos(base), np.sin(base), np.cos(local), np.sin(local)))


def kernel(x, norm_gain, w_in, b_merge, sb_q_gain, sb_k_gain, ret_out_gain,
           w_branch_sb, w_branch_ret, w_out):
    batch, seq, d_model = x.shape
    depth = norm_gain.shape[0]
    assert d_model == D_MODEL and w_in.shape[-1] == N_SEGMENTS * D_MODEL
    assert seq % TM == 0 and TM % CHUNK == 0
    x2 = x.reshape(batch * seq, D_MODEL)
    for layer in range(depth):
        gain = norm_gain[layer][None, :]
        split = IN_SEGMENTS * D_MODEL
        qt, k, vt, sg, rq, rk, rv, rg = _inproj(
            x2, gain, w_in[layer][:, :split].astype(BF16),
            sb_q_gain[layer][None, :], sb_k_gain[layer][None, :], seq)
        x2 = _branches_merge(
            qt, k, vt, sg, rq, rk, rv, rg, x2, gain,
            w_in[layer][:, split:].astype(BF16), b_merge[layer].reshape(1, 2 * D_MODEL),
            ret_out_gain[layer],
            w_branch_sb[layer].astype(BF16), w_branch_ret[layer].astype(BF16),
            w_out[layer].astype(BF16), seq)
    return x2.reshape(batch, seq, D_MODEL)
```

```python
import functools

import numpy as np
import jax
import jax.numpy as jnp
from jax import lax
from jax.experimental import pallas as pl
from jax.experimental.pallas import tpu as pltpu

D_MODEL = 1024
SB_HEADS = 8
SB_HEAD_DIM = D_MODEL // SB_HEADS
RET_HEADS = 4
RET_DIM = D_MODEL // RET_HEADS
CHUNK = 64
ROPE_BASE = 10000.0
EPS = 1e-6
N_SEGMENTS = 10
IN_SEGMENTS = 8
LOG2E = 1.4426950408889634

VMEM_LIMIT_BYTES = 56 * 1024 * 1024

TM = 256
IN_TILES = 2
SB_DONE_LOG2 = 160.0

F32 = jnp.float32
BF16 = jnp.bfloat16


def _rms(x, gain):
    y = x * lax.rsqrt(jnp.mean(x * x, axis=-1, keepdims=True) + EPS)
    return y * gain


def _silu(g):
    return g * jax.nn.sigmoid(g)


def _inproj_kernel(x_ref, ng_ref, w_ref, qg_ref, kg_ref, cosa_ref, sina_ref, cosb_ref,
                   sinb_ref, qt_ref, k_ref, vt_ref, *row_out_refs):
    for sub in range(IN_TILES):
        rows = pl.ds(sub * TM, TM)
        _inproj_tile(x_ref.at[rows], ng_ref, w_ref, qg_ref, kg_ref, cosa_ref.at[sub],
                     sina_ref.at[sub], cosb_ref, sinb_ref, qt_ref.at[sub], k_ref.at[rows],
                     vt_ref.at[sub], *[r.at[rows] for r in row_out_refs])


def _inproj_tile(x_ref, ng_ref, w_ref, qg_ref, kg_ref, cosa_ref, sina_ref, cosb_ref,
                 sinb_ref, qt_ref, k_ref, vt_ref, sg_ref, rq_ref, rk_ref, rv_ref, rg_ref):
    hb = _rms(x_ref[...], ng_ref[...]).astype(BF16)

    def seg(s):
        return jnp.dot(hb, w_ref[:, s * D_MODEL:(s + 1) * D_MODEL],
                       preferred_element_type=F32)

    def heads(p, out_ref, gain_ref=None, post_scale=None, transpose=False):
        for h in range(SB_HEADS):
            sl = slice(h * SB_HEAD_DIM, (h + 1) * SB_HEAD_DIM)
            y = p[:, sl]
            if gain_ref is not None:
                y = _rms(y, gain_ref[...])
            if post_scale is not None:
                y = y * post_scale
            if transpose:
                out_ref[sl, :] = y.T.astype(BF16)
            else:
                out_ref[:, sl] = y.astype(BF16)

    ca, sa = cosa_ref[...], sina_ref[...]
    cb, sb = cosb_ref[...], sinb_ref[...]
    cos = ca * cb - sa * sb
    sin = sa * cb + ca * sb

    def rotary(p, out_ref, post_scale):
        half = RET_DIM // 2
        for h in range(RET_HEADS):
            t1 = p[:, h * RET_DIM:h * RET_DIM + half]
            t2 = p[:, h * RET_DIM + half:(h + 1) * RET_DIM]
            o1 = t1 * cos - t2 * sin
            o2 = t1 * sin + t2 * cos
            if post_scale is not None:
                o1 = o1 * post_scale
                o2 = o2 * post_scale
            out_ref[:, h * RET_DIM:h * RET_DIM + half] = o1.astype(BF16)
            out_ref[:, h * RET_DIM + half:(h + 1) * RET_DIM] = o2.astype(BF16)

    heads(seg(0), qt_ref, qg_ref, (SB_HEAD_DIM ** -0.5) * LOG2E, transpose=True)
    heads(seg(1), k_ref, kg_ref)
    heads(seg(2), vt_ref, transpose=True)
    sg_ref[...] = _silu(seg(3))
    rotary(seg(4), rq_ref, None)
    rotary(seg(5), rk_ref, RET_DIM ** -0.5)
    rg_ref[...] = _silu(seg(7))
    rv_ref[...] = seg(6).astype(BF16)


def _inproj(x2, norm_gain, w_in_bf, q_gain, k_gain, seq):
    m = x2.shape[0]
    rows = IN_TILES * TM
    steps_per_seq = seq // rows
    cosa, sina, cosb, sinb = _rope_tables(seq, TM)
    row = lambda i: (i, 0)
    const = lambda i: (0, 0)
    tile_base = pl.BlockSpec((IN_TILES, 1, RET_DIM // 2),
                             lambda i: (i % steps_per_seq, 0, 0))
    bf_out = jax.ShapeDtypeStruct((m, D_MODEL), BF16)
    f32_out = jax.ShapeDtypeStruct((m, D_MODEL), F32)
    out_spec = pl.BlockSpec((rows, D_MODEL), row)
    t_out = jax.ShapeDtypeStruct((m // TM, D_MODEL, TM), BF16)
    t_spec = pl.BlockSpec((IN_TILES, D_MODEL, TM), lambda i: (i, 0, 0))
    return pl.pallas_call(
        _inproj_kernel,
        grid=(m // rows,),
        in_specs=[
            pl.BlockSpec((rows, D_MODEL), row),
            pl.BlockSpec((1, D_MODEL), const),
            pl.BlockSpec((D_MODEL, IN_SEGMENTS * D_MODEL), const),
            pl.BlockSpec((1, SB_HEAD_DIM), const),
            pl.BlockSpec((1, SB_HEAD_DIM), const),
            tile_base, tile_base,
            pl.BlockSpec((TM, RET_DIM // 2), const),
            pl.BlockSpec((TM, RET_DIM // 2), const),
        ],
        out_specs=[t_spec, out_spec, t_spec] + [out_spec] * 5,
        out_shape=[t_out, bf_out, t_out, f32_out, bf_out, bf_out, bf_out, f32_out],
        compiler_params=pltpu.CompilerParams(
            dimension_semantics=("arbitrary",),
            vmem_limit_bytes=VMEM_LIMIT_BYTES),
        name="inproj",
    )(x2, norm_gain, w_in_bf, q_gain, k_gain, cosa, sina, cosb, sinb)


def _softplus2(z2):
    return jnp.maximum(z2, 0.0) + jnp.log(1.0 + jnp.exp2(-jnp.abs(z2))) * LOG2E


def _bm_kernel(qt_ref, kc_ref, kp_ref, vtc_ref, vtp_ref, sg_ref,
               rq_ref, rk_ref, rv_ref, rg_ref, dec_ref, qd_ref, kd_ref, gl_ref, og_ref,
               xc_ref, ng_ref, wm_ref, bm_ref, x_ref, wsb_ref, wret_ref, wout_ref, tri_ref,
               k_hbm, vt_hbm,
               o_ref,
               oa_ref, ob_ref, gate_ref, hb_ref, mg_ref, state_ref, rs_ref, rq_state_ref,
               z_ref, sp_ref, acc_ref, carry_ref, kbuf_ref, vbuf_ref, *, tiles_per_seq):
    m = pl.program_id(0)
    cur = jnp.minimum(m, pl.num_programs(0) - 2)
    t = cur % tiles_per_seq
    has_prev = t > 0
    slot = m % 2
    tri = tri_ref[...]

    @pl.when(m == 0)
    def _():
        def zero_rows(r, _):
            rows = pl.ds(pl.multiple_of(r * 16, 16), 16)
            oa_ref[1, rows, :] = jnp.zeros((16, D_MODEL), BF16)
            ob_ref[1, rows, :] = jnp.zeros((16, D_MODEL), BF16)
            gate_ref[1, rows, :] = jnp.zeros((16, 2 * D_MODEL), F32)
            return 0

        lax.fori_loop(0, TM // 16, zero_rows, 0)

    def head_cols(h):
        return slice(h * SB_HEAD_DIM, (h + 1) * SB_HEAD_DIM)

    def causal_mask():
        row = lax.broadcasted_iota(jnp.int32, (TM, TM), 0)
        col = lax.broadcasted_iota(jnp.int32, (TM, TM), 1)
        return row < col

    def write_head(h):
        oa_ref[slot, :, head_cols(h)] = (
            acc_ref[h].T * sg_ref[:, head_cols(h)]).astype(BF16)

    n_piece = D_MODEL // TM

    def gate_piece(j):
        cols = slice(j * TM, (j + 1) * TM)
        g = jnp.dot(hb_ref[...], wm_ref[:, cols], preferred_element_type=F32)
        gate_ref[slot, :, cols] = jax.nn.sigmoid(g + bm_ref[:, cols])

    def merged_piece(j):
        cols = slice(j * TM, (j + 1) * TM)
        gcols = slice(D_MODEL + j * TM, D_MODEL + (j + 1) * TM)
        pa = jnp.dot(oa_ref[1 - slot], wsb_ref[:, cols], preferred_element_type=F32)
        pb = jnp.dot(ob_ref[1 - slot], wret_ref[:, cols], preferred_element_type=F32)
        mg_ref[:, cols] = (gate_ref[1 - slot, :, cols] * pa
                           + gate_ref[1 - slot, :, gcols] * pb).astype(BF16)

    def out_piece(j):
        cols = slice(j * TM, (j + 1) * TM)
        o_ref[:, cols] = x_ref[:, cols] + jnp.dot(mg_ref[...], wout_ref[:, cols],
                                                  preferred_element_type=F32)

    def ret_cols(h):
        return slice(h * RET_DIM, (h + 1) * RET_DIM)

    def retention_scores(h):
        q = rq_ref[:, ret_cols(h)]
        k = rk_ref[:, ret_cols(h)]
        v = rv_ref[:, ret_cols(h)]
        rs_ref[h] = lax.dot_general(q, k, (((1,), (1,)), ((), ())),
                                    preferred_element_type=F32)
        state = jnp.where(has_prev, state_ref[h], 0.0)
        rq_state_ref[h] = jnp.dot(q, state.astype(BF16), preferred_element_type=F32)
        kdt = (k.astype(F32) * kd_ref[h]).T.astype(BF16)
        state_ref[h] = state * gl_ref[h] + jnp.dot(kdt, v, preferred_element_type=F32)

    def retention_out(h):
        p = (rs_ref[h] * dec_ref[h]).astype(BF16)
        o = jnp.dot(p, rv_ref[:, ret_cols(h)], preferred_element_type=F32)
        o = o + qd_ref[h] * rq_state_ref[h]
        ob_ref[slot, :, ret_cols(h)] = (
            _rms(o, og_ref[h]) * rg_ref[:, ret_cols(h)]).astype(BF16)

    units = [(h, d) for d in (0, 1) for h in range(SB_HEADS)]
    n_units = len(units)
    n_gate = 2 * n_piece
    hb_ref[...] = _rms(xc_ref[...], ng_ref[...]).astype(BF16)
    for u, (h, d) in enumerate(units):
        kt = (kc_ref if d == 0 else kp_ref)[:, head_cols(h)]
        z_ref[u] = jnp.dot(kt, qt_ref[head_cols(h), :], preferred_element_type=F32)
        if (u + 1) % (n_units // RET_HEADS) == 0:
            retention_scores(u // (n_units // RET_HEADS))
        if (u + 3) % (n_units // (n_gate // 2)) == 0:
            gate_piece(u // (n_units // (n_gate // 2)))
    for u, (h, d) in enumerate(units):
        if u % (n_units // n_piece) == 0:
            merged_piece(u // (n_units // n_piece))
        sp2 = _softplus2(z_ref[u])
        sp2 = jnp.where(causal_mask() if d == 0 else has_prev, sp2, 0.0)
        sp_ref[u] = sp2.astype(BF16)
    total = {}

    def cumsum(h):
        for u in (h, SB_HEADS + h):
            cs = jnp.dot(tri, sp_ref[u], preferred_element_type=F32)
            z_ref[u] = z_ref[u] - cs
            total[u] = cs[0:1, :]

    ahead = 4
    for h in range(ahead):
        cumsum(h)
    for h in range(SB_HEADS):
        if h + ahead < SB_HEADS:
            cumsum(h + ahead)
        if h % (SB_HEADS // RET_HEADS) == 0:
            retention_out(h // (SB_HEADS // RET_HEADS))
        else:
            gate_piece(n_gate // 2 + h // (SB_HEADS // RET_HEADS))
        w = jnp.where(causal_mask(), jnp.exp2(z_ref[h]), 0.0)
        carry = total[h]
        pv = jnp.dot(vtc_ref[head_cols(h), :], w.astype(BF16), preferred_element_type=F32)
        w = jnp.where(has_prev, jnp.exp2(z_ref[SB_HEADS + h] - carry), 0.0)
        pv = pv + jnp.dot(vtp_ref[head_cols(h), :], w.astype(BF16),
                          preferred_element_type=F32)
        acc_ref[h] = pv
        carry_ref[h] = carry + total[SB_HEADS + h]
        write_head(h)
        if (h + 1) % (SB_HEADS // n_piece) == 0:
            out_piece(h // (SB_HEADS // n_piece))

    def unfinished(h):
        return jnp.min(carry_ref[h]) < SB_DONE_LOG2

    @pl.when(jnp.logical_and(t >= 2, jnp.min(carry_ref[...]) < SB_DONE_LOG2))
    def _():
        for h in range(SB_HEADS):
            def more(state):
                d, go = state
                return jnp.logical_and(d <= t, go)

            def sweep(state, h=h):
                d, _ = state
                tile = cur - d
                row0 = pl.multiple_of(tile * TM, TM)
                pltpu.sync_copy(k_hbm.at[pl.ds(row0, TM), head_cols(h)], kbuf_ref)
                pltpu.sync_copy(vt_hbm.at[tile, head_cols(h), :], vbuf_ref)
                z2 = jnp.dot(kbuf_ref[...], qt_ref[head_cols(h), :],
                             preferred_element_type=F32)
                cs = jnp.dot(tri, _softplus2(z2).astype(BF16), preferred_element_type=F32)
                carry = carry_ref[h]
                w = jnp.exp2(z2 - cs - carry)
                carry_ref[h] = carry + cs[0:1, :]
                acc_ref[h] += jnp.dot(vbuf_ref[...], w.astype(BF16),
                                      preferred_element_type=F32)
                return d + 1, unfinished(h)

            lax.while_loop(more, sweep, (jnp.int32(2), unfinished(h)))
            write_head(h)


def _retention_tables():
    log_gamma = np.log1p(-np.exp2(-5.0 - np.arange(RET_HEADS)))
    t = np.arange(TM, dtype=np.float64)
    ct = np.arange(TM) // CHUNK
    dist = t[:, None] - t[None, :]
    lg = log_gamma[:, None, None]
    same = (ct[:, None] == ct[None, :])[None]
    earlier = (ct[None, :] < ct[:, None])[None]
    decay = np.where(same, np.exp(lg * np.abs(dist)[None]),
                     np.where(earlier, np.exp(lg * dist[None]), 0.0))
    ones = np.ones((1, 1, RET_DIM))
    qdec = np.exp(log_gamma[:, None] * (t + 1.0)[None, :])[:, :, None] * ones
    kdec = np.exp(log_gamma[:, None] * (TM - 1.0 - t)[None, :])[:, :, None] * ones
    gl = np.exp(log_gamma * TM)[:, None, None] * ones
    return tuple(jnp.asarray(a, F32) for a in (decay, qdec, kdec, gl))


def _branches_merge(qt, k, vt, sg, rq, rk, rv, rg, x2, norm_gain, w_gate, b_gate,
                    ret_out_gain, wsb, wret, wout, seq):
    m = x2.shape[0]
    n_tiles = m // TM
    tiles_per_seq = seq // TM
    last = n_tiles - 1
    tri = (jnp.arange(TM)[None, :] >= jnp.arange(TM)[:, None]).astype(BF16)
    decay, qdec, kdec, gl = _retention_tables()
    gain = ret_out_gain.reshape(RET_HEADS, 1, RET_DIM)

    cur = lambda i: jnp.minimum(i, last)
    row_cur = lambda i: (cur(i), 0)
    row_prev_tile = lambda i: (jnp.maximum(cur(i) - 1, 0), 0)
    row_lag = lambda i: (jnp.maximum(i - 1, 0), 0)
    t_cur = lambda i: (cur(i), 0, 0)
    t_prev = lambda i: (jnp.maximum(cur(i) - 1, 0), 0, 0)
    full2 = lambda i: (0, 0)
    full3 = lambda i: (0, 0, 0)

    rows = lambda imap: pl.BlockSpec((TM, D_MODEL), imap)
    trans = lambda imap: pl.BlockSpec((None, D_MODEL, TM), imap)
    table = lambda a: pl.BlockSpec(a.shape, full3)
    weight = pl.BlockSpec((D_MODEL, D_MODEL), full2)
    hbm = pl.BlockSpec(memory_space=pl.ANY)
    return pl.pallas_call(
        functools.partial(_bm_kernel, tiles_per_seq=tiles_per_seq),
        grid=(n_tiles + 1,),
        in_specs=[
            trans(t_cur), rows(row_cur), rows(row_prev_tile), trans(t_cur), trans(t_prev),
            rows(row_cur),
            rows(row_cur), rows(row_cur), rows(row_cur), rows(row_cur),
            table(decay), table(qdec), table(kdec), table(gl), table(gain),
            rows(row_cur), pl.BlockSpec((1, D_MODEL), full2),
            pl.BlockSpec((D_MODEL, 2 * D_MODEL), full2), pl.BlockSpec((1, 2 * D_MODEL), full2),
            rows(row_lag), weight, weight, weight,
            pl.BlockSpec((TM, TM), full2),
            hbm, hbm,
        ],
        out_specs=rows(row_lag),
        out_shape=jax.ShapeDtypeStruct((m, D_MODEL), F32),
        scratch_shapes=[
            pltpu.VMEM((2, TM, D_MODEL), BF16),
            pltpu.VMEM((2, TM, D_MODEL), BF16),
            pltpu.VMEM((2, TM, 2 * D_MODEL), F32),
            pltpu.VMEM((TM, D_MODEL), BF16),
            pltpu.VMEM((TM, D_MODEL), BF16),
            pltpu.VMEM((RET_HEADS, RET_DIM, RET_DIM), F32),
            pltpu.VMEM((RET_HEADS, TM, TM), F32),
            pltpu.VMEM((RET_HEADS, TM, RET_DIM), F32),
            pltpu.VMEM((2 * SB_HEADS, TM, TM), F32),
            pltpu.VMEM((2 * SB_HEADS, TM, TM), BF16),
            pltpu.VMEM((SB_HEADS, SB_HEAD_DIM, TM), F32),
            pltpu.VMEM((SB_HEADS, 1, TM), F32),
            pltpu.VMEM((TM, SB_HEAD_DIM), BF16),
            pltpu.VMEM((SB_HEAD_DIM, TM), BF16),
        ],
        compiler_params=pltpu.CompilerParams(
            dimension_semantics=("arbitrary",),
            vmem_limit_bytes=VMEM_LIMIT_BYTES),
        name="branches_merge",
    )(qt, k, k, vt, vt, sg, rq, rk, rv, rg, decay, qdec, kdec, gl, gain,
      x2, norm_gain, w_gate, b_gate, x2, wsb, wret, wout, tri, k, vt)


def _rope_tables(seq, tile):
    d = RET_DIM
    inv_freq = ROPE_BASE ** (-np.arange(0, d, 2, dtype=np.float64) / d)
    base = (np.arange(seq // tile, dtype=np.float64) * tile)[:, None, None] * inv_freq
    local = np.arange(tile, dtype=np.float64)[:, None] * inv_freq
    return tuple(jnp.asarray(a, F32)
                 for a in (np.cos(base), np.sin(base), np.cos(local), np.sin(local)))


def kernel(x, norm_gain, w_in, b_merge, sb_q_gain, sb_k_gain, ret_out_gain,
           w_branch_sb, w_branch_ret, w_out):
    batch, seq, d_model = x.shape
    depth = norm_gain.shape[0]
    assert d_model == D_MODEL and w_in.shape[-1] == N_SEGMENTS * D_MODEL
    assert seq % (IN_TILES * TM) == 0 and TM % CHUNK == 0
    x2 = x.reshape(batch * seq, D_MODEL)
    for layer in range(depth):
        gain = norm_gain[layer][None, :]
        split = IN_SEGMENTS * D_MODEL
        qt, k, vt, sg, rq, rk, rv, rg = _inproj(
            x2, gain, w_in[layer][:, :split].astype(BF16),
            sb_q_gain[layer][None, :], sb_k_gain[layer][None, :], seq)
        x2 = _branches_merge(
            qt, k, vt, sg, rq, rk, rv, rg, x2, gain,
            w_in[layer][:, split:].astype(BF16), b_merge[layer].reshape(1, 2 * D_MODEL),
            ret_out_gain[layer],
            w_branch_sb[layer].astype(BF16), w_branch_ret[layer].astype(BF16),
            w_out[layer].astype(BF16), seq)
    return x2.reshape(batch, seq, D_MODEL)
```

```python
import functools

import numpy as np
import jax
import jax.numpy as jnp
from jax import lax
from jax.experimental import pallas as pl
from jax.experimental.pallas import tpu as pltpu

D_MODEL = 1024
SB_HEADS = 8
SB_HEAD_DIM = D_MODEL // SB_HEADS
RET_HEADS = 4
RET_DIM = D_MODEL // RET_HEADS
CHUNK = 64
ROPE_BASE = 10000.0
EPS = 1e-6
N_SEGMENTS = 10
IN_SEGMENTS = 8
LOG2E = 1.4426950408889634

VMEM_LIMIT_BYTES = 56 * 1024 * 1024

TM = 256
IN_TILES = 2
SB_DONE_LOG2 = 160.0

F32 = jnp.float32
BF16 = jnp.bfloat16


def _rms(x, gain):
    y = x * lax.rsqrt(jnp.mean(x * x, axis=-1, keepdims=True) + EPS)
    return y * gain


def _silu(g):
    return g * jax.nn.sigmoid(g)


def _inproj_kernel(x_ref, ng_ref, w_ref, qg_ref, kg_ref, cosa_ref, sina_ref, cosb_ref,
                   sinb_ref, qt_ref, k_ref, vt_ref, *row_out_refs):
    for sub in range(IN_TILES):
        rows = pl.ds(sub * TM, TM)
        _inproj_tile(x_ref.at[rows], ng_ref, w_ref, qg_ref, kg_ref, cosa_ref.at[sub],
                     sina_ref.at[sub], cosb_ref, sinb_ref, qt_ref.at[sub], k_ref.at[rows],
                     vt_ref.at[sub], *[r.at[rows] for r in row_out_refs])


def _inproj_tile(x_ref, ng_ref, w_ref, qg_ref, kg_ref, cosa_ref, sina_ref, cosb_ref,
                 sinb_ref, qt_ref, k_ref, vt_ref, sg_ref, rq_ref, rk_ref, rv_ref, rg_ref):
    hb = _rms(x_ref[...], ng_ref[...]).astype(BF16)

    def seg(s):
        return jnp.dot(hb, w_ref[:, s * D_MODEL:(s + 1) * D_MODEL],
                       preferred_element_type=F32)

    def heads(p, out_ref, gain_ref=None, post_scale=None, transpose=False):
        for h in range(SB_HEADS):
            sl = slice(h * SB_HEAD_DIM, (h + 1) * SB_HEAD_DIM)
            y = p[:, sl]
            if gain_ref is not None:
                y = _rms(y, gain_ref[...])
            if post_scale is not None:
                y = y * post_scale
            if transpose:
                out_ref[sl, :] = y.T.astype(BF16)
            else:
                out_ref[:, sl] = y.astype(BF16)

    ca, sa = cosa_ref[...], sina_ref[...]
    cb, sb = cosb_ref[...], sinb_ref[...]
    cos = ca * cb - sa * sb
    sin = sa * cb + ca * sb

    def rotary(p, out_ref, post_scale):
        half = RET_DIM // 2
        for h in range(RET_HEADS):
            t1 = p[:, h * RET_DIM:h * RET_DIM + half]
            t2 = p[:, h * RET_DIM + half:(h + 1) * RET_DIM]
            o1 = t1 * cos - t2 * sin
            o2 = t1 * sin + t2 * cos
            if post_scale is not None:
                o1 = o1 * post_scale
                o2 = o2 * post_scale
            out_ref[:, h * RET_DIM:h * RET_DIM + half] = o1.astype(BF16)
            out_ref[:, h * RET_DIM + half:(h + 1) * RET_DIM] = o2.astype(BF16)

    heads(seg(0), qt_ref, qg_ref, (SB_HEAD_DIM ** -0.5) * LOG2E, transpose=True)
    heads(seg(1), k_ref, kg_ref)
    heads(seg(2), vt_ref, transpose=True)
    sg_ref[...] = _silu(seg(3))
    rotary(seg(4), rq_ref, None)
    rotary(seg(5), rk_ref, RET_DIM ** -0.5)
    rg_ref[...] = _silu(seg(7))
    rv_ref[...] = seg(6).astype(BF16)


def _inproj(x2, norm_gain, w_in_bf, q_gain, k_gain, seq):
    m = x2.shape[0]
    rows = IN_TILES * TM
    steps_per_seq = seq // rows
    cosa, sina, cosb, sinb = _rope_tables(seq, TM)
    row = lambda i: (i, 0)
    const = lambda i: (0, 0)
    tile_base = pl.BlockSpec((IN_TILES, 1, RET_DIM // 2),
                             lambda i: (i % steps_per_seq, 0, 0))
    bf_out = jax.ShapeDtypeStruct((m, D_MODEL), BF16)
    f32_out = jax.ShapeDtypeStruct((m, D_MODEL), F32)
    out_spec = pl.BlockSpec((rows, D_MODEL), row)
    t_out = jax.ShapeDtypeStruct((m // TM, D_MODEL, TM), BF16)
    t_spec = pl.BlockSpec((IN_TILES, D_MODEL, TM), lambda i: (i, 0, 0))
    return pl.pallas_call(
        _inproj_kernel,
        grid=(m // rows,),
        in_specs=[
            pl.BlockSpec((rows, D_MODEL), row),
            pl.BlockSpec((1, D_MODEL), const),
            pl.BlockSpec((D_MODEL, IN_SEGMENTS * D_MODEL), const,
                         pipeline_mode=pl.Buffered(1)),
            pl.BlockSpec((1, SB_HEAD_DIM), const),
            pl.BlockSpec((1, SB_HEAD_DIM), const),
            tile_base, tile_base,
            pl.BlockSpec((TM, RET_DIM // 2), const),
            pl.BlockSpec((TM, RET_DIM // 2), const),
        ],
        out_specs=[t_spec, out_spec, t_spec] + [out_spec] * 5,
        out_shape=[t_out, bf_out, t_out, f32_out, bf_out, bf_out, bf_out, f32_out],
        compiler_params=pltpu.CompilerParams(
            dimension_semantics=("arbitrary",),
            vmem_limit_bytes=VMEM_LIMIT_BYTES),
        name="inproj",
    )(x2, norm_gain, w_in_bf, q_gain, k_gain, cosa, sina, cosb, sinb)


def _softplus2(z2):
    return jnp.maximum(z2, 0.0) + jnp.log(1.0 + jnp.exp2(-jnp.abs(z2))) * LOG2E


def _bm_kernel(qt_ref, kc_ref, kp_ref, vtc_ref, vtp_ref, sg_ref,
               rq_ref, rk_ref, rv_ref, rg_ref, dec_ref, qd_ref, kd_ref, gl_ref, og_ref,
               xc_ref, ng_ref, wm_ref, bm_ref, x_ref, wsb_ref, wret_ref, wout_ref, tri_ref,
               k_hbm, vt_hbm,
               o_ref,
               oa_ref, ob_ref, gate_ref, hb_ref, mg_ref, state_ref, rs_ref, rq_state_ref,
               z_ref, sp_ref, acc_ref, carry_ref, kbuf_ref, vbuf_ref, *, tiles_per_seq):
    m = pl.program_id(0)
    cur = jnp.minimum(m, pl.num_programs(0) - 2)
    t = cur % tiles_per_seq
    has_prev = t > 0
    slot = m % 2
    tri = tri_ref[...]

    @pl.when(m == 0)
    def _():
        def zero_rows(r, _):
            rows = pl.ds(pl.multiple_of(r * 16, 16), 16)
            oa_ref[1, rows, :] = jnp.zeros((16, D_MODEL), BF16)
            ob_ref[1, rows, :] = jnp.zeros((16, D_MODEL), BF16)
            gate_ref[1, rows, :] = jnp.zeros((16, 2 * D_MODEL), F32)
            return 0

        lax.fori_loop(0, TM // 16, zero_rows, 0)

    def head_cols(h):
        return slice(h * SB_HEAD_DIM, (h + 1) * SB_HEAD_DIM)

    def causal_mask():
        row = lax.broadcasted_iota(jnp.int32, (TM, TM), 0)
        col = lax.broadcasted_iota(jnp.int32, (TM, TM), 1)
        return row < col

    def write_head(h):
        oa_ref[slot, :, head_cols(h)] = (
            acc_ref[h].T * sg_ref[:, head_cols(h)]).astype(BF16)

    n_piece = D_MODEL // TM

    def gate_piece(j):
        cols = slice(j * TM, (j + 1) * TM)
        g = jnp.dot(hb_ref[...], wm_ref[:, cols], preferred_element_type=F32)
        gate_ref[slot, :, cols] = jax.nn.sigmoid(g + bm_ref[:, cols])

    def merged_piece(j):
        cols = slice(j * TM, (j + 1) * TM)
        gcols = slice(D_MODEL + j * TM, D_MODEL + (j + 1) * TM)
        pa = jnp.dot(oa_ref[1 - slot], wsb_ref[:, cols], preferred_element_type=F32)
        pb = jnp.dot(ob_ref[1 - slot], wret_ref[:, cols], preferred_element_type=F32)
        mg_ref[:, cols] = (gate_ref[1 - slot, :, cols] * pa
                           + gate_ref[1 - slot, :, gcols] * pb).astype(BF16)

    def out_piece(j):
        cols = slice(j * TM, (j + 1) * TM)
        o_ref[:, cols] = x_ref[:, cols] + jnp.dot(mg_ref[...], wout_ref[:, cols],
                                                  preferred_element_type=F32)

    def ret_cols(h):
        return slice(h * RET_DIM, (h + 1) * RET_DIM)

    def retention_scores(h):
        q = rq_ref[:, ret_cols(h)]
        k = rk_ref[:, ret_cols(h)]
        v = rv_ref[:, ret_cols(h)]
        rs_ref[h] = lax.dot_general(q, k, (((1,), (1,)), ((), ())),
                                    preferred_element_type=F32)
        state = jnp.where(has_prev, state_ref[h], 0.0)
        rq_state_ref[h] = jnp.dot(q, state.astype(BF16), preferred_element_type=F32)
        kdt = (k.astype(F32) * kd_ref[h]).T.astype(BF16)
        state_ref[h] = state * gl_ref[h] + jnp.dot(kdt, v, preferred_element_type=F32)

    def retention_out(h):
        p = (rs_ref[h] * dec_ref[h]).astype(BF16)
        o = jnp.dot(p, rv_ref[:, ret_cols(h)], preferred_element_type=F32)
        o = o + qd_ref[h] * rq_state_ref[h]
        ob_ref[slot, :, ret_cols(h)] = (
            _rms(o, og_ref[h]) * rg_ref[:, ret_cols(h)]).astype(BF16)

    units = [(h, d) for d in (0, 1) for h in range(SB_HEADS)]
    n_units = len(units)
    n_gate = 2 * n_piece
    hb_ref[...] = _rms(xc_ref[...], ng_ref[...]).astype(BF16)
    for u, (h, d) in enumerate(units):
        kt = (kc_ref if d == 0 else kp_ref)[:, head_cols(h)]
        z_ref[u] = jnp.dot(kt, qt_ref[head_cols(h), :], preferred_element_type=F32)
        if (u + 1) % (n_units // RET_HEADS) == 0:
            retention_scores(u // (n_units // RET_HEADS))
        if (u + 3) % (n_units // (n_gate // 2)) == 0:
            gate_piece(u // (n_units // (n_gate // 2)))
    for u, (h, d) in enumerate(units):
        if u % (n_units // n_piece) == 0:
            merged_piece(u // (n_units // n_piece))
        sp2 = _softplus2(z_ref[u])
        sp2 = jnp.where(causal_mask() if d == 0 else has_prev, sp2, 0.0)
        sp_ref[u] = sp2.astype(BF16)
    total = {}

    def cumsum(h):
        for u in (h, SB_HEADS + h):
            cs = jnp.dot(tri, sp_ref[u], preferred_element_type=F32)
            z_ref[u] = z_ref[u] - cs
            total[u] = cs[0:1, :]

    ahead = 4
    for h in range(ahead):
        cumsum(h)
    for h in range(SB_HEADS):
        if h + ahead < SB_HEADS:
            cumsum(h + ahead)
        if h % (SB_HEADS // RET_HEADS) == 0:
            retention_out(h // (SB_HEADS // RET_HEADS))
        else:
            gate_piece(n_gate // 2 + h // (SB_HEADS // RET_HEADS))
        w = jnp.where(causal_mask(), jnp.exp2(z_ref[h]), 0.0)
        carry = total[h]
        pv = jnp.dot(vtc_ref[head_cols(h), :], w.astype(BF16), preferred_element_type=F32)
        w = jnp.where(has_prev, jnp.exp2(z_ref[SB_HEADS + h] - carry), 0.0)
        pv = pv + jnp.dot(vtp_ref[head_cols(h), :], w.astype(BF16),
                          preferred_element_type=F32)
        acc_ref[h] = pv
        carry_ref[h] = carry + total[SB_HEADS + h]
        write_head(h)
        if (h + 1) % (SB_HEADS // n_piece) == 0:
            out_piece(h // (SB_HEADS // n_piece))

    def unfinished(h):
        return jnp.min(carry_ref[h]) < SB_DONE_LOG2

    @pl.when(jnp.logical_and(t >= 2, jnp.min(carry_ref[...]) < SB_DONE_LOG2))
    def _():
        for h in range(SB_HEADS):
            def more(state):
                d, go = state
                return jnp.logical_and(d <= t, go)

            def sweep(state, h=h):
                d, _ = state
                tile = cur - d
                row0 = pl.multiple_of(tile * TM, TM)
                pltpu.sync_copy(k_hbm.at[pl.ds(row0, TM), head_cols(h)], kbuf_ref)
                pltpu.sync_copy(vt_hbm.at[tile, head_cols(h), :], vbuf_ref)
                z2 = jnp.dot(kbuf_ref[...], qt_ref[head_cols(h), :],
                             preferred_element_type=F32)
                cs = jnp.dot(tri, _softplus2(z2).astype(BF16), preferred_element_type=F32)
                carry = carry_ref[h]
                w = jnp.exp2(z2 - cs - carry)
                carry_ref[h] = carry + cs[0:1, :]
                acc_ref[h] += jnp.dot(vbuf_ref[...], w.astype(BF16),
                                      preferred_element_type=F32)
                return d + 1, unfinished(h)

            lax.while_loop(more, sweep, (jnp.int32(2), unfinished(h)))
            write_head(h)


def _retention_tables():
    log_gamma = np.log1p(-np.exp2(-5.0 - np.arange(RET_HEADS)))
    t = np.arange(TM, dtype=np.float64)
    ct = np.arange(TM) // CHUNK
    dist = t[:, None] - t[None, :]
    lg = log_gamma[:, None, None]
    same = (ct[:, None] == ct[None, :])[None]
    earlier = (ct[None, :] < ct[:, None])[None]
    decay = np.where(same, np.exp(lg * np.abs(dist)[None]),
                     np.where(earlier, np.exp(lg * dist[None]), 0.0))
    ones = np.ones((1, 1, RET_DIM))
    qdec = np.exp(log_gamma[:, None] * (t + 1.0)[None, :])[:, :, None] * ones
    kdec = np.exp(log_gamma[:, None] * (TM - 1.0 - t)[None, :])[:, :, None] * ones
    gl = np.exp(log_gamma * TM)[:, None, None] * ones
    return tuple(jnp.asarray(a, F32) for a in (decay, qdec, kdec, gl))


def _branches_merge(qt, k, vt, sg, rq, rk, rv, rg, x2, norm_gain, w_gate, b_gate,
                    ret_out_gain, wsb, wret, wout, seq):
    m = x2.shape[0]
    n_tiles = m // TM
    tiles_per_seq = seq // TM
    last = n_tiles - 1
    tri = (jnp.arange(TM)[None, :] >= jnp.arange(TM)[:, None]).astype(BF16)
    decay, qdec, kdec, gl = _retention_tables()
    gain = ret_out_gain.reshape(RET_HEADS, 1, RET_DIM)

    cur = lambda i: jnp.minimum(i, last)
    row_cur = lambda i: (cur(i), 0)
    row_prev_tile = lambda i: (jnp.maximum(cur(i) - 1, 0), 0)
    row_lag = lambda i: (jnp.maximum(i - 1, 0), 0)
    t_cur = lambda i: (cur(i), 0, 0)
    t_prev = lambda i: (jnp.maximum(cur(i) - 1, 0), 0, 0)
    full2 = lambda i: (0, 0)
    full3 = lambda i: (0, 0, 0)

    rows = lambda imap: pl.BlockSpec((TM, D_MODEL), imap)
    trans = lambda imap: pl.BlockSpec((None, D_MODEL, TM), imap)
    table = lambda a: pl.BlockSpec(a.shape, full3)
    weight = pl.BlockSpec((D_MODEL, D_MODEL), full2)
    hbm = pl.BlockSpec(memory_space=pl.ANY)
    return pl.pallas_call(
        functools.partial(_bm_kernel, tiles_per_seq=tiles_per_seq),
        grid=(n_tiles + 1,),
        in_specs=[
            trans(t_cur), rows(row_cur), rows(row_prev_tile), trans(t_cur), trans(t_prev),
            rows(row_cur),
            rows(row_cur), rows(row_cur), rows(row_cur), rows(row_cur),
            table(decay), table(qdec), table(kdec), table(gl), table(gain),
            rows(row_cur), pl.BlockSpec((1, D_MODEL), full2),
            pl.BlockSpec((D_MODEL, 2 * D_MODEL), lambda i: (0, IN_SEGMENTS // 2),
                         pipeline_mode=pl.Buffered(1)),
            pl.BlockSpec((1, 2 * D_MODEL), full2),
            rows(row_lag), weight, weight, weight,
            pl.BlockSpec((TM, TM), full2),
            hbm, hbm,
        ],
        out_specs=rows(row_lag),
        out_shape=jax.ShapeDtypeStruct((m, D_MODEL), F32),
        scratch_shapes=[
            pltpu.VMEM((2, TM, D_MODEL), BF16),
            pltpu.VMEM((2, TM, D_MODEL), BF16),
            pltpu.VMEM((2, TM, 2 * D_MODEL), F32),
            pltpu.VMEM((TM, D_MODEL), BF16),
            pltpu.VMEM((TM, D_MODEL), BF16),
            pltpu.VMEM((RET_HEADS, RET_DIM, RET_DIM), F32),
            pltpu.VMEM((RET_HEADS, TM, TM), F32),
            pltpu.VMEM((RET_HEADS, TM, RET_DIM), F32),
            pltpu.VMEM((2 * SB_HEADS, TM, TM), F32),
            pltpu.VMEM((2 * SB_HEADS, TM, TM), BF16),
            pltpu.VMEM((SB_HEADS, SB_HEAD_DIM, TM), F32),
            pltpu.VMEM((SB_HEADS, 1, TM), F32),
            pltpu.VMEM((TM, SB_HEAD_DIM), BF16),
            pltpu.VMEM((SB_HEAD_DIM, TM), BF16),
        ],
        compiler_params=pltpu.CompilerParams(
            dimension_semantics=("arbitrary",),
            vmem_limit_bytes=VMEM_LIMIT_BYTES),
        name="branches_merge",
    )(qt, k, k, vt, vt, sg, rq, rk, rv, rg, decay, qdec, kdec, gl, gain,
      x2, norm_gain, w_gate, b_gate, x2, wsb, wret, wout, tri, k, vt)


def _rope_tables(seq, tile):
    d = RET_DIM
    inv_freq = ROPE_BASE ** (-np.arange(0, d, 2, dtype=np.float64) / d)
    base = (np.arange(seq // tile, dtype=np.float64) * tile)[:, None, None] * inv_freq
    local = np.arange(tile, dtype=np.float64)[:, None] * inv_freq
    return tuple(jnp.asarray(a, F32)
                 for a in (np.cos(base), np.sin(base), np.cos(local), np.sin(local)))


def kernel(x, norm_gain, w_in, b_merge, sb_q_gain, sb_k_gain, ret_out_gain,
           w_branch_sb, w_branch_ret, w_out):
    batch, seq, d_model = x.shape
    depth = norm_gain.shape[0]
    assert d_model == D_MODEL and w_in.shape[-1] == N_SEGMENTS * D_MODEL
    assert seq % (IN_TILES * TM) == 0 and TM % CHUNK == 0
    x2 = x.reshape(batch * seq, D_MODEL)
    for layer in range(depth):
        gain = norm_gain[layer][None, :]
        w_in_bf = w_in[layer].astype(BF16)
        qt, k, vt, sg, rq, rk, rv, rg = _inproj(
            x2, gain, w_in_bf, sb_q_gain[layer][None, :], sb_k_gain[layer][None, :], seq)
        x2 = _branches_merge(
            qt, k, vt, sg, rq, rk, rv, rg, x2, gain,
            w_in_bf, b_merge[layer].reshape(1, 2 * D_MODEL),
            ret_out_gain[layer],
            w_branch_sb[layer].astype(BF16), w_branch_ret[layer].astype(BF16),
            w_out[layer].astype(BF16), seq)
    return x2.reshape(batch, seq, D_MODEL)
```

```python
import functools

import numpy as np
import jax
import jax.numpy as jnp
from jax import lax
from jax.experimental import pallas as pl
from jax.experimental.pallas import tpu as pltpu

D_MODEL = 1024
SB_HEADS = 8
SB_HEAD_DIM = D_MODEL // SB_HEADS
RET_HEADS = 4
RET_DIM = D_MODEL // RET_HEADS
CHUNK = 64
ROPE_BASE = 10000.0
EPS = 1e-6
N_SEGMENTS = 10
IN_SEGMENTS = 8
LOG2E = 1.4426950408889634

VMEM_LIMIT_BYTES = 56 * 1024 * 1024

TM = 256
IN_TILES = 2
SB_DONE_LOG2 = 160.0

F32 = jnp.float32
BF16 = jnp.bfloat16


def _rms(x, gain):
    y = x * lax.rsqrt(jnp.mean(x * x, axis=-1, keepdims=True) + EPS)
    return y * gain


def _silu(g):
    return g * jax.nn.sigmoid(g)


def _inproj_kernel(x_ref, ng_ref, w_ref, qg_ref, kg_ref, cosa_ref, sina_ref, cosb_ref,
                   sinb_ref, qt_ref, k_ref, vt_ref, *row_out_refs):
    for sub in range(IN_TILES):
        rows = pl.ds(sub * TM, TM)
        _inproj_tile(x_ref.at[rows], ng_ref, w_ref, qg_ref, kg_ref, cosa_ref.at[sub],
                     sina_ref.at[sub], cosb_ref, sinb_ref, qt_ref.at[sub], k_ref.at[rows],
                     vt_ref.at[sub], *[r.at[rows] for r in row_out_refs])


def _inproj_tile(x_ref, ng_ref, w_ref, qg_ref, kg_ref, cosa_ref, sina_ref, cosb_ref,
                 sinb_ref, qt_ref, k_ref, vt_ref, sg_ref, rq_ref, rk_ref, rv_ref, rg_ref):
    hb = _rms(x_ref[...], ng_ref[...]).astype(BF16)

    def seg(s):
        return jnp.dot(hb, w_ref[:, s * D_MODEL:(s + 1) * D_MODEL],
                       preferred_element_type=F32)

    def heads(p, out_ref, gain_ref=None, post_scale=None, transpose=False):
        for h in range(SB_HEADS):
            sl = slice(h * SB_HEAD_DIM, (h + 1) * SB_HEAD_DIM)
            y = p[:, sl]
            if gain_ref is not None:
                y = _rms(y, gain_ref[...])
            if post_scale is not None:
                y = y * post_scale
            if transpose:
                out_ref[sl, :] = y.T.astype(BF16)
            else:
                out_ref[:, sl] = y.astype(BF16)

    ca, sa = cosa_ref[...], sina_ref[...]
    cb, sb = cosb_ref[...], sinb_ref[...]
    cos = ca * cb - sa * sb
    sin = sa * cb + ca * sb

    def rotary(p, out_ref, post_scale):
        half = RET_DIM // 2
        for h in range(RET_HEADS):
            t1 = p[:, h * RET_DIM:h * RET_DIM + half]
            t2 = p[:, h * RET_DIM + half:(h + 1) * RET_DIM]
            o1 = t1 * cos - t2 * sin
            o2 = t1 * sin + t2 * cos
            if post_scale is not None:
                o1 = o1 * post_scale
                o2 = o2 * post_scale
            out_ref[:, h * RET_DIM:h * RET_DIM + half] = o1.astype(BF16)
            out_ref[:, h * RET_DIM + half:(h + 1) * RET_DIM] = o2.astype(BF16)

    heads(seg(0), qt_ref, qg_ref, (SB_HEAD_DIM ** -0.5) * LOG2E, transpose=True)
    heads(seg(1), k_ref, kg_ref)
    heads(seg(2), vt_ref, transpose=True)
    sg_ref[...] = _silu(seg(3))
    rotary(seg(4), rq_ref, None)
    rotary(seg(5), rk_ref, RET_DIM ** -0.5)
    rg_ref[...] = _silu(seg(7))
    rv_ref[...] = seg(6).astype(BF16)


def _inproj(x2, norm_gain, w_in_bf, q_gain, k_gain, seq):
    m = x2.shape[0]
    rows = IN_TILES * TM
    steps_per_seq = seq // rows
    cosa, sina, cosb, sinb = _rope_tables(seq, TM)
    row = lambda i: (i, 0)
    const = lambda i: (0, 0)
    tile_base = pl.BlockSpec((IN_TILES, 1, RET_DIM // 2),
                             lambda i: (i % steps_per_seq, 0, 0))
    bf_out = jax.ShapeDtypeStruct((m, D_MODEL), BF16)
    f32_out = jax.ShapeDtypeStruct((m, D_MODEL), F32)
    out_spec = pl.BlockSpec((rows, D_MODEL), row)
    t_out = jax.ShapeDtypeStruct((m // TM, D_MODEL, TM), BF16)
    t_spec = pl.BlockSpec((IN_TILES, D_MODEL, TM), lambda i: (i, 0, 0))
    return pl.pallas_call(
        _inproj_kernel,
        grid=(m // rows,),
        in_specs=[
            pl.BlockSpec((rows, D_MODEL), row),
            pl.BlockSpec((1, D_MODEL), const),
            pl.BlockSpec((D_MODEL, IN_SEGMENTS * D_MODEL), const,
                         pipeline_mode=pl.Buffered(1)),
            pl.BlockSpec((1, SB_HEAD_DIM), const),
            pl.BlockSpec((1, SB_HEAD_DIM), const),
            tile_base, tile_base,
            pl.BlockSpec((TM, RET_DIM // 2), const),
            pl.BlockSpec((TM, RET_DIM // 2), const),
        ],
        out_specs=[t_spec, out_spec, t_spec] + [out_spec] * 5,
        out_shape=[t_out, bf_out, t_out, f32_out, bf_out, bf_out, bf_out, f32_out],
        compiler_params=pltpu.CompilerParams(
            dimension_semantics=("arbitrary",),
            vmem_limit_bytes=VMEM_LIMIT_BYTES),
        name="inproj",
    )(x2, norm_gain, w_in_bf, q_gain, k_gain, cosa, sina, cosb, sinb)


def _softplus2(z2):
    return jnp.maximum(z2, 0.0) + jnp.log(1.0 + jnp.exp2(-jnp.abs(z2))) * LOG2E


def _bm_kernel(qt_ref, kc_ref, kp_ref, vtc_ref, vtp_ref, sg_ref,
               rq_ref, rk_ref, rv_ref, rg_ref, dec_ref, qd_ref, kd_ref, gl_ref, og_ref,
               xc_ref, ng_ref, wm_ref, bm_ref, x_ref, wsb_ref, wret_ref, wout_ref, tri_ref,
               k_hbm, vt_hbm,
               o_ref,
               oa_ref, ob_ref, gate_ref, hb_ref, mg_ref, state_ref, rs_ref, rq_state_ref,
               z_ref, sp_ref, acc_ref, carry_ref, kbuf_ref, vbuf_ref, *, tiles_per_seq):
    m = pl.program_id(0)
    cur = jnp.minimum(m, pl.num_programs(0) - 2)
    t = cur % tiles_per_seq
    has_prev = t > 0
    slot = m % 2
    tri = tri_ref[...]

    @pl.when(m == 0)
    def _():
        def zero_rows(r, _):
            rows = pl.ds(pl.multiple_of(r * 16, 16), 16)
            oa_ref[1, rows, :] = jnp.zeros((16, D_MODEL), BF16)
            ob_ref[1, rows, :] = jnp.zeros((16, D_MODEL), BF16)
            gate_ref[1, rows, :] = jnp.zeros((16, 2 * D_MODEL), F32)
            return 0

        lax.fori_loop(0, TM // 16, zero_rows, 0)

    def head_cols(h):
        return slice(h * SB_HEAD_DIM, (h + 1) * SB_HEAD_DIM)

    def causal_mask():
        row = lax.broadcasted_iota(jnp.int32, (TM, TM), 0)
        col = lax.broadcasted_iota(jnp.int32, (TM, TM), 1)
        return row < col

    def write_head(h):
        oa_ref[slot, :, head_cols(h)] = (
            acc_ref[h].T * sg_ref[:, head_cols(h)]).astype(BF16)

    n_piece = D_MODEL // TM

    def gate_piece(j):
        cols = slice(j * TM, (j + 1) * TM)
        g = jnp.dot(hb_ref[...], wm_ref[:, cols], preferred_element_type=F32)
        gate_ref[slot, :, cols] = jax.nn.sigmoid(g + bm_ref[:, cols])

    def merged_piece(j):
        cols = slice(j * TM, (j + 1) * TM)
        gcols = slice(D_MODEL + j * TM, D_MODEL + (j + 1) * TM)
        pa = jnp.dot(oa_ref[1 - slot], wsb_ref[:, cols], preferred_element_type=F32)
        pb = jnp.dot(ob_ref[1 - slot], wret_ref[:, cols], preferred_element_type=F32)
        mg_ref[:, cols] = (gate_ref[1 - slot, :, cols] * pa
                           + gate_ref[1 - slot, :, gcols] * pb).astype(BF16)

    def out_piece(j):
        cols = slice(j * TM, (j + 1) * TM)
        o_ref[:, cols] = x_ref[:, cols] + jnp.dot(mg_ref[...], wout_ref[:, cols],
                                                  preferred_element_type=F32)

    def ret_cols(h):
        return slice(h * RET_DIM, (h + 1) * RET_DIM)

    def retention_scores(h):
        q = rq_ref[:, ret_cols(h)]
        k = rk_ref[:, ret_cols(h)]
        v = rv_ref[:, ret_cols(h)]
        rs_ref[h] = lax.dot_general(q, k, (((1,), (1,)), ((), ())),
                                    preferred_element_type=F32)
        state = jnp.where(has_prev, state_ref[h], 0.0)
        rq_state_ref[h] = jnp.dot(q, state.astype(BF16), preferred_element_type=F32)
        kdt = (k.astype(F32) * kd_ref[h]).T.astype(BF16)
        state_ref[h] = state * gl_ref[h] + jnp.dot(kdt, v, preferred_element_type=F32)

    def retention_out(h):
        p = (rs_ref[h] * dec_ref[h]).astype(BF16)
        o = jnp.dot(p, rv_ref[:, ret_cols(h)], preferred_element_type=F32)
        o = o + qd_ref[h] * rq_state_ref[h]
        ob_ref[slot, :, ret_cols(h)] = (
            _rms(o, og_ref[h]) * rg_ref[:, ret_cols(h)]).astype(BF16)

    units = [(h, d) for d in (0, 1) for h in range(SB_HEADS)]
    n_units = len(units)
    n_gate = 2 * n_piece
    hb_ref[...] = _rms(xc_ref[...], ng_ref[...]).astype(BF16)
    for u, (h, d) in enumerate(units):
        kt = (kc_ref if d == 0 else kp_ref)[:, head_cols(h)]
        z_ref[u] = jnp.dot(kt, qt_ref[head_cols(h), :], preferred_element_type=F32)
        if (u + 1) % (n_units // RET_HEADS) == 0:
            retention_scores(u // (n_units // RET_HEADS))
        if (u + 3) % (n_units // (n_gate // 2)) == 0:
            gate_piece(u // (n_units // (n_gate // 2)))
    for u, (h, d) in enumerate(units):
        if u % (n_units // n_piece) == 0:
            merged_piece(u // (n_units // n_piece))
        sp2 = _softplus2(z_ref[u])
        sp2 = jnp.where(causal_mask() if d == 0 else has_prev, sp2, 0.0)
        sp_ref[u] = sp2.astype(BF16)
    total = {}

    def cumsum(h):
        for u in (h, SB_HEADS + h):
            cs = jnp.dot(tri, sp_ref[u], preferred_element_type=F32)
            z_ref[u] = z_ref[u] - cs
            total[u] = cs[0:1, :]

    ahead = 4
    for h in range(ahead):
        cumsum(h)
    for h in range(SB_HEADS):
        if h + ahead < SB_HEADS:
            cumsum(h + ahead)
        if h % (SB_HEADS // RET_HEADS) == 0:
            retention_out(h // (SB_HEADS // RET_HEADS))
        else:
            gate_piece(n_gate // 2 + h // (SB_HEADS // RET_HEADS))
        w = jnp.where(causal_mask(), jnp.exp2(z_ref[h]), 0.0)
        carry = total[h]
        pv = jnp.dot(vtc_ref[head_cols(h), :], w.astype(BF16), preferred_element_type=F32)
        w = jnp.where(has_prev, jnp.exp2(z_ref[SB_HEADS + h] - carry), 0.0)
        pv = pv + jnp.dot(vtp_ref[head_cols(h), :], w.astype(BF16),
                          preferred_element_type=F32)
        acc_ref[h] = pv
        carry_ref[h] = carry + total[SB_HEADS + h]
        write_head(h)
        if (h + 1) % (SB_HEADS // n_piece) == 0:
            out_piece(h // (SB_HEADS // n_piece))

    def unfinished():
        return jnp.min(carry_ref[...]) < SB_DONE_LOG2

    @pl.when(jnp.logical_and(t >= 2, unfinished()))
    def _():
        def more(state):
            d, go = state
            return jnp.logical_and(d <= t, go)

        def sweep(state):
            d, _ = state
            tile = cur - d
            row0 = pl.multiple_of(tile * TM, TM)
            pltpu.sync_copy(k_hbm.at[pl.ds(row0, TM), :], kbuf_ref)
            pltpu.sync_copy(vt_hbm.at[tile], vbuf_ref)
            for h in range(SB_HEADS):
                z2 = jnp.dot(kbuf_ref[:, head_cols(h)], qt_ref[head_cols(h), :],
                             preferred_element_type=F32)
                cs = jnp.dot(tri, _softplus2(z2).astype(BF16), preferred_element_type=F32)
                carry = carry_ref[h]
                w = jnp.exp2(z2 - cs - carry)
                carry_ref[h] = carry + cs[0:1, :]
                acc_ref[h] += jnp.dot(vbuf_ref[head_cols(h), :], w.astype(BF16),
                                      preferred_element_type=F32)
            return d + 1, unfinished()

        lax.while_loop(more, sweep, (jnp.int32(2), True))
        for h in range(SB_HEADS):
            write_head(h)


def _retention_tables():
    log_gamma = np.log1p(-np.exp2(-5.0 - np.arange(RET_HEADS)))
    t = np.arange(TM, dtype=np.float64)
    ct = np.arange(TM) // CHUNK
    dist = t[:, None] - t[None, :]
    lg = log_gamma[:, None, None]
    same = (ct[:, None] == ct[None, :])[None]
    earlier = (ct[None, :] < ct[:, None])[None]
    decay = np.where(same, np.exp(lg * np.abs(dist)[None]),
                     np.where(earlier, np.exp(lg * dist[None]), 0.0))
    ones = np.ones((1, 1, RET_DIM))
    qdec = np.exp(log_gamma[:, None] * (t + 1.0)[None, :])[:, :, None] * ones
    kdec = np.exp(log_gamma[:, None] * (TM - 1.0 - t)[None, :])[:, :, None] * ones
    gl = np.exp(log_gamma * TM)[:, None, None] * ones
    return tuple(jnp.asarray(a, F32) for a in (decay, qdec, kdec, gl))


def _branches_merge(qt, k, vt, sg, rq, rk, rv, rg, x2, norm_gain, w_gate, b_gate,
                    ret_out_gain, wsb, wret, wout, seq):
    m = x2.shape[0]
    n_tiles = m // TM
    tiles_per_seq = seq // TM
    last = n_tiles - 1
    tri = (jnp.arange(TM)[None, :] >= jnp.arange(TM)[:, None]).astype(BF16)
    decay, qdec, kdec, gl = _retention_tables()
    gain = ret_out_gain.reshape(RET_HEADS, 1, RET_DIM)

    cur = lambda i: jnp.minimum(i, last)
    row_cur = lambda i: (cur(i), 0)
    row_prev_tile = lambda i: (jnp.maximum(cur(i) - 1, 0), 0)
    row_lag = lambda i: (jnp.maximum(i - 1, 0), 0)
    t_cur = lambda i: (cur(i), 0, 0)
    t_prev = lambda i: (jnp.maximum(cur(i) - 1, 0), 0, 0)
    full2 = lambda i: (0, 0)
    full3 = lambda i: (0, 0, 0)

    rows = lambda imap: pl.BlockSpec((TM, D_MODEL), imap)
    trans = lambda imap: pl.BlockSpec((None, D_MODEL, TM), imap)
    table = lambda a: pl.BlockSpec(a.shape, full3)
    weight = pl.BlockSpec((D_MODEL, D_MODEL), full2)
    hbm = pl.BlockSpec(memory_space=pl.ANY)
    return pl.pallas_call(
        functools.partial(_bm_kernel, tiles_per_seq=tiles_per_seq),
        grid=(n_tiles + 1,),
        in_specs=[
            trans(t_cur), rows(row_cur), rows(row_prev_tile), trans(t_cur), trans(t_prev),
            rows(row_cur),
            rows(row_cur), rows(row_cur), rows(row_cur), rows(row_cur),
            table(decay), table(qdec), table(kdec), table(gl), table(gain),
            rows(row_cur), pl.BlockSpec((1, D_MODEL), full2),
            pl.BlockSpec((D_MODEL, 2 * D_MODEL), lambda i: (0, IN_SEGMENTS // 2),
                         pipeline_mode=pl.Buffered(1)),
            pl.BlockSpec((1, 2 * D_MODEL), full2),
            rows(row_lag), weight, weight, weight,
            pl.BlockSpec((TM, TM), full2),
            hbm, hbm,
        ],
        out_specs=rows(row_lag),
        out_shape=jax.ShapeDtypeStruct((m, D_MODEL), F32),
        scratch_shapes=[
            pltpu.VMEM((2, TM, D_MODEL), BF16),
            pltpu.VMEM((2, TM, D_MODEL), BF16),
            pltpu.VMEM((2, TM, 2 * D_MODEL), F32),
            pltpu.VMEM((TM, D_MODEL), BF16),
            pltpu.VMEM((TM, D_MODEL), BF16),
            pltpu.VMEM((RET_HEADS, RET_DIM, RET_DIM), F32),
            pltpu.VMEM((RET_HEADS, TM, TM), F32),
            pltpu.VMEM((RET_HEADS, TM, RET_DIM), F32),
            pltpu.VMEM((2 * SB_HEADS, TM, TM), F32),
            pltpu.VMEM((2 * SB_HEADS, TM, TM), BF16),
            pltpu.VMEM((SB_HEADS, SB_HEAD_DIM, TM), F32),
            pltpu.VMEM((SB_HEADS, 1, TM), F32),
            pltpu.VMEM((TM, D_MODEL), BF16),
            pltpu.VMEM((D_MODEL, TM), BF16),
        ],
        compiler_params=pltpu.CompilerParams(
            dimension_semantics=("arbitrary",),
            vmem_limit_bytes=VMEM_LIMIT_BYTES),
        name="branches_merge",
    )(qt, k, k, vt, vt, sg, rq, rk, rv, rg, decay, qdec, kdec, gl, gain,
      x2, norm_gain, w_gate, b_gate, x2, wsb, wret, wout, tri, k, vt)


def _rope_tables(seq, tile):
    d = RET_DIM
    inv_freq = ROPE_BASE ** (-np.arange(0, d, 2, dtype=np.float64) / d)
    base = (np.arange(seq // tile, dtype=np.float64) * tile)[:, None, None] * inv_freq
    local = np.arange(tile, dtype=np.float64)[:, None] * inv_freq
    return tuple(jnp.asarray(a, F32)
                 for a in (np.cos(base), np.sin(base), np.cos(local), np.sin(local)))


def kernel(x, norm_gain, w_in, b_merge, sb_q_gain, sb_k_gain, ret_out_gain,
           w_branch_sb, w_branch_ret, w_out):
    batch, seq, d_model = x.shape
    depth = norm_gain.shape[0]
    assert d_model == D_MODEL and w_in.shape[-1] == N_SEGMENTS * D_MODEL
    assert seq % (IN_TILES * TM) == 0 and TM % CHUNK == 0
    x2 = x.reshape(batch * seq, D_MODEL)
    for layer in range(depth):
        gain = norm_gain[layer][None, :]
        w_in_bf = w_in[layer].astype(BF16)
        qt, k, vt, sg, rq, rk, rv, rg = _inproj(
            x2, gain, w_in_bf, sb_q_gain[layer][None, :], sb_k_gain[layer][None, :], seq)
        x2 = _branches_merge(
            qt, k, vt, sg, rq, rk, rv, rg, x2, gain,
            w_in_bf, b_merge[layer].reshape(1, 2 * D_MODEL),
            ret_out_gain[layer],
            w_branch_sb[layer].astype(BF16), w_branch_ret[layer].astype(BF16),
            w_out[layer].astype(BF16), seq)
    return x2.reshape(batch, seq, D_MODEL)
```

```python
import functools

import numpy as np
import jax
import jax.numpy as jnp
from jax import lax
from jax.experimental import pallas as pl
from jax.experimental.pallas import tpu as pltpu

D_MODEL = 1024
SB_HEADS = 8
SB_HEAD_DIM = D_MODEL // SB_HEADS
RET_HEADS = 4
RET_DIM = D_MODEL // RET_HEADS
CHUNK = 64
ROPE_BASE = 10000.0
EPS = 1e-6
N_SEGMENTS = 10
IN_SEGMENTS = 8
LOG2E = 1.4426950408889634

VMEM_LIMIT_BYTES = 56 * 1024 * 1024

TM = 256
IN_TILES = 2
SB_DONE_LOG2 = 160.0

F32 = jnp.float32
BF16 = jnp.bfloat16


def _rms(x, gain):
    y = x * lax.rsqrt(jnp.mean(x * x, axis=-1, keepdims=True) + EPS)
    return y * gain


def _silu(g):
    return g * jax.nn.sigmoid(g)


def _inproj_kernel(x_ref, ng_ref, w_ref, qg_ref, kg_ref, cosa_ref, sina_ref, cosb_ref,
                   sinb_ref, qt_ref, k_ref, vt_ref, *row_out_refs):
    for sub in range(IN_TILES):
        rows = pl.ds(sub * TM, TM)
        _inproj_tile(x_ref.at[rows], ng_ref, w_ref, qg_ref, kg_ref, cosa_ref.at[sub],
                     sina_ref.at[sub], cosb_ref, sinb_ref, qt_ref.at[sub], k_ref.at[rows],
                     vt_ref.at[sub], *[r.at[rows] for r in row_out_refs])


def _inproj_tile(x_ref, ng_ref, w_ref, qg_ref, kg_ref, cosa_ref, sina_ref, cosb_ref,
                 sinb_ref, qt_ref, k_ref, vt_ref, sg_ref, rq_ref, rk_ref, rv_ref, rg_ref):
    hb = _rms(x_ref[...], ng_ref[...]).astype(BF16)

    def seg(s):
        return jnp.dot(hb, w_ref[:, s * D_MODEL:(s + 1) * D_MODEL],
                       preferred_element_type=F32)

    def heads(p, out_ref, gain_ref=None, post_scale=None, transpose=False):
        for h in range(SB_HEADS):
            sl = slice(h * SB_HEAD_DIM, (h + 1) * SB_HEAD_DIM)
            y = p[:, sl]
            if gain_ref is not None:
                y = _rms(y, gain_ref[...])
            if post_scale is not None:
                y = y * post_scale
            if transpose:
                out_ref[sl, :] = y.T.astype(BF16)
            else:
                out_ref[:, sl] = y.astype(BF16)

    ca, sa = cosa_ref[...], sina_ref[...]
    cb, sb = cosb_ref[...], sinb_ref[...]
    cos = ca * cb - sa * sb
    sin = sa * cb + ca * sb

    def rotary(p, out_ref, post_scale):
        half = RET_DIM // 2
        for h in range(RET_HEADS):
            t1 = p[:, h * RET_DIM:h * RET_DIM + half]
            t2 = p[:, h * RET_DIM + half:(h + 1) * RET_DIM]
            o1 = t1 * cos - t2 * sin
            o2 = t1 * sin + t2 * cos
            if post_scale is not None:
                o1 = o1 * post_scale
                o2 = o2 * post_scale
            out_ref[:, h * RET_DIM:h * RET_DIM + half] = o1.astype(BF16)
            out_ref[:, h * RET_DIM + half:(h + 1) * RET_DIM] = o2.astype(BF16)

    heads(seg(0), qt_ref, qg_ref, (SB_HEAD_DIM ** -0.5) * LOG2E, transpose=True)
    heads(seg(1), k_ref, kg_ref)
    heads(seg(2), vt_ref, transpose=True)
    sg_ref[...] = _silu(seg(3))
    rotary(seg(4), rq_ref, None)
    rotary(seg(5), rk_ref, RET_DIM ** -0.5)
    rg_ref[...] = _silu(seg(7))
    rv_ref[...] = seg(6).astype(BF16)


def _inproj(x2, norm_gain, w_in_bf, q_gain, k_gain, seq):
    m = x2.shape[0]
    rows = IN_TILES * TM
    steps_per_seq = seq // rows
    cosa, sina, cosb, sinb = _rope_tables(seq, TM)
    row = lambda i: (i, 0)
    const = lambda i: (0, 0)
    tile_base = pl.BlockSpec((IN_TILES, 1, RET_DIM // 2),
                             lambda i: (i % steps_per_seq, 0, 0))
    bf_out = jax.ShapeDtypeStruct((m, D_MODEL), BF16)
    f32_out = jax.ShapeDtypeStruct((m, D_MODEL), F32)
    out_spec = pl.BlockSpec((rows, D_MODEL), row)
    t_out = jax.ShapeDtypeStruct((m // TM, D_MODEL, TM), BF16)
    t_spec = pl.BlockSpec((IN_TILES, D_MODEL, TM), lambda i: (i, 0, 0))
    return pl.pallas_call(
        _inproj_kernel,
        grid=(m // rows,),
        in_specs=[
            pl.BlockSpec((rows, D_MODEL), row),
            pl.BlockSpec((1, D_MODEL), const),
            pl.BlockSpec((D_MODEL, IN_SEGMENTS * D_MODEL), const,
                         pipeline_mode=pl.Buffered(1)),
            pl.BlockSpec((1, SB_HEAD_DIM), const),
            pl.BlockSpec((1, SB_HEAD_DIM), const),
            tile_base, tile_base,
            pl.BlockSpec((TM, RET_DIM // 2), const),
            pl.BlockSpec((TM, RET_DIM // 2), const),
        ],
        out_specs=[t_spec, out_spec, t_spec] + [out_spec] * 5,
        out_shape=[t_out, bf_out, t_out, f32_out, bf_out, bf_out, bf_out, f32_out],
        compiler_params=pltpu.CompilerParams(
            dimension_semantics=("arbitrary",),
            vmem_limit_bytes=VMEM_LIMIT_BYTES),
        name="inproj",
    )(x2, norm_gain, w_in_bf, q_gain, k_gain, cosa, sina, cosb, sinb)


def _softplus2(z2):
    return jnp.maximum(z2, 0.0) + jnp.log(1.0 + jnp.exp2(-jnp.abs(z2))) * LOG2E


def _bm_kernel(qt_ref, kc_ref, kp_ref, vtc_ref, vtp_ref, sg_ref,
               rq_ref, rk_ref, rv_ref, rg_ref, dec_ref, qd_ref, kd_ref, gl_ref, og_ref,
               xc_ref, ng_ref, wm_ref, bm_ref, x_ref, wsb_ref, wret_ref, wout_ref, tri_ref,
               k_hbm, vt_hbm,
               o_ref,
               oa_ref, ob_ref, gate_ref, hb_ref, mg_ref, state_ref, rs_ref, rq_state_ref,
               z_ref, sp_ref, acc_ref, carry_ref, kbuf_ref, vbuf_ref, *, tiles_per_seq):
    m = pl.program_id(0)
    cur = jnp.minimum(m, pl.num_programs(0) - 2)
    t = cur % tiles_per_seq
    has_prev = t > 0
    slot = m % 2
    tri = tri_ref[...]

    @pl.when(m == 0)
    def _():
        def zero_rows(r, _):
            rows = pl.ds(pl.multiple_of(r * 16, 16), 16)
            oa_ref[1, rows, :] = jnp.zeros((16, D_MODEL), BF16)
            ob_ref[1, rows, :] = jnp.zeros((16, D_MODEL), BF16)
            gate_ref[1, rows, :] = jnp.zeros((16, 2 * D_MODEL), F32)
            return 0

        lax.fori_loop(0, TM // 16, zero_rows, 0)

    def head_cols(h):
        return slice(h * SB_HEAD_DIM, (h + 1) * SB_HEAD_DIM)

    def causal_mask():
        row = lax.broadcasted_iota(jnp.int32, (TM, TM), 0)
        col = lax.broadcasted_iota(jnp.int32, (TM, TM), 1)
        return row < col

    def write_head(h):
        oa_ref[slot, :, head_cols(h)] = (
            acc_ref[h].T * sg_ref[:, head_cols(h)]).astype(BF16)

    n_piece = D_MODEL // TM

    def gate_piece(j):
        cols = slice(j * TM, (j + 1) * TM)
        g = jnp.dot(hb_ref[...], wm_ref[:, cols], preferred_element_type=F32)
        gate_ref[slot, :, cols] = jax.nn.sigmoid(g + bm_ref[:, cols])

    def merged_piece(j):
        cols = slice(j * TM, (j + 1) * TM)
        gcols = slice(D_MODEL + j * TM, D_MODEL + (j + 1) * TM)
        pa = jnp.dot(oa_ref[1 - slot], wsb_ref[:, cols], preferred_element_type=F32)
        pb = jnp.dot(ob_ref[1 - slot], wret_ref[:, cols], preferred_element_type=F32)
        mg_ref[:, cols] = (gate_ref[1 - slot, :, cols] * pa
                           + gate_ref[1 - slot, :, gcols] * pb).astype(BF16)

    def out_piece(j):
        cols = slice(j * TM, (j + 1) * TM)
        o_ref[:, cols] = x_ref[:, cols] + jnp.dot(mg_ref[...], wout_ref[:, cols],
                                                  preferred_element_type=F32)

    def ret_cols(h):
        return slice(h * RET_DIM, (h + 1) * RET_DIM)

    def retention_scores(h):
        q = rq_ref[:, ret_cols(h)]
        k = rk_ref[:, ret_cols(h)]
        v = rv_ref[:, ret_cols(h)]
        rs_ref[h] = lax.dot_general(q, k, (((1,), (1,)), ((), ())),
                                    preferred_element_type=F32)
        state = jnp.where(has_prev, state_ref[h], 0.0)
        rq_state_ref[h] = jnp.dot(q, state.astype(BF16), preferred_element_type=F32)
        kdt = (k.astype(F32) * kd_ref[h]).T.astype(BF16)
        state_ref[h] = state * gl_ref[h] + jnp.dot(kdt, v, preferred_element_type=F32)

    def retention_out(h):
        p = (rs_ref[h] * dec_ref[h]).astype(BF16)
        o = jnp.dot(p, rv_ref[:, ret_cols(h)], preferred_element_type=F32)
        o = o + qd_ref[h] * rq_state_ref[h]
        ob_ref[slot, :, ret_cols(h)] = (
            _rms(o, og_ref[h]) * rg_ref[:, ret_cols(h)]).astype(BF16)

    units = [(h, d) for d in (0, 1) for h in range(SB_HEADS)]
    n_units = len(units)
    n_gate = 2 * n_piece
    hb_ref[...] = _rms(xc_ref[...], ng_ref[...]).astype(BF16)
    for u, (h, d) in enumerate(units):
        kt = (kc_ref if d == 0 else kp_ref)[:, head_cols(h)]
        z_ref[u] = jnp.dot(kt, qt_ref[head_cols(h), :], preferred_element_type=F32)
        if u % (n_units // RET_HEADS) == 0:
            retention_scores(u // (n_units // RET_HEADS))
        if (u + 3) % (n_units // (n_gate // 2)) == 0:
            gate_piece(u // (n_units // (n_gate // 2)))
    for u, (h, d) in enumerate(units):
        if u % (n_units // n_piece) == 0:
            merged_piece(u // (n_units // n_piece))
        sp2 = _softplus2(z_ref[u])
        sp2 = jnp.where(causal_mask() if d == 0 else has_prev, sp2, 0.0)
        sp_ref[u] = sp2.astype(BF16)
    total = {}

    def cumsum(h):
        for u in (h, SB_HEADS + h):
            cs = jnp.dot(tri, sp_ref[u], preferred_element_type=F32)
            z_ref[u] = z_ref[u] - cs
            total[u] = cs[0:1, :]

    ahead = 4
    for h in range(ahead):
        cumsum(h)
    for h in range(SB_HEADS):
        if h + ahead < SB_HEADS:
            cumsum(h + ahead)
        if h % (SB_HEADS // RET_HEADS) == 0:
            retention_out(h // (SB_HEADS // RET_HEADS))
        else:
            gate_piece(n_gate // 2 + h // (SB_HEADS // RET_HEADS))
        w = jnp.where(causal_mask(), jnp.exp2(z_ref[h]), 0.0)
        carry = total[h]
        pv = jnp.dot(vtc_ref[head_cols(h), :], w.astype(BF16), preferred_element_type=F32)
        w = jnp.where(has_prev, jnp.exp2(z_ref[SB_HEADS + h] - carry), 0.0)
        pv = pv + jnp.dot(vtp_ref[head_cols(h), :], w.astype(BF16),
                          preferred_element_type=F32)
        acc_ref[h] = pv
        carry_ref[h] = carry + total[SB_HEADS + h]
        write_head(h)
        if (h + 1) % (SB_HEADS // n_piece) == 0:
            out_piece(h // (SB_HEADS // n_piece))

    def unfinished():
        return jnp.min(carry_ref[...]) < SB_DONE_LOG2

    @pl.when(jnp.logical_and(t >= 2, unfinished()))
    def _():
        def more(state):
            d, go = state
            return jnp.logical_and(d <= t, go)

        def sweep(state):
            d, _ = state
            tile = cur - d
            row0 = pl.multiple_of(tile * TM, TM)
            pltpu.sync_copy(k_hbm.at[pl.ds(row0, TM), :], kbuf_ref)
            pltpu.sync_copy(vt_hbm.at[tile], vbuf_ref)
            for h in range(SB_HEADS):
                z2 = jnp.dot(kbuf_ref[:, head_cols(h)], qt_ref[head_cols(h), :],
                             preferred_element_type=F32)
                cs = jnp.dot(tri, _softplus2(z2).astype(BF16), preferred_element_type=F32)
                carry = carry_ref[h]
                w = jnp.exp2(z2 - cs - carry)
                carry_ref[h] = carry + cs[0:1, :]
                acc_ref[h] += jnp.dot(vbuf_ref[head_cols(h), :], w.astype(BF16),
                                      preferred_element_type=F32)
            return d + 1, unfinished()

        lax.while_loop(more, sweep, (jnp.int32(2), True))
        for h in range(SB_HEADS):
            write_head(h)


def _retention_tables():
    log_gamma = np.log1p(-np.exp2(-5.0 - np.arange(RET_HEADS)))
    t = np.arange(TM, dtype=np.float64)
    ct = np.arange(TM) // CHUNK
    dist = t[:, None] - t[None, :]
    lg = log_gamma[:, None, None]
    same = (ct[:, None] == ct[None, :])[None]
    earlier = (ct[None, :] < ct[:, None])[None]
    decay = np.where(same, np.exp(lg * np.abs(dist)[None]),
                     np.where(earlier, np.exp(lg * dist[None]), 0.0))
    ones = np.ones((1, 1, RET_DIM))
    qdec = np.exp(log_gamma[:, None] * (t + 1.0)[None, :])[:, :, None] * ones
    kdec = np.exp(log_gamma[:, None] * (TM - 1.0 - t)[None, :])[:, :, None] * ones
    gl = np.exp(log_gamma * TM)[:, None, None] * ones
    return tuple(jnp.asarray(a, F32) for a in (decay, qdec, kdec, gl))


def _branches_merge(qt, k, vt, sg, rq, rk, rv, rg, x2, norm_gain, w_gate, b_gate,
                    ret_out_gain, wsb, wret, wout, seq):
    m = x2.shape[0]
    n_tiles = m // TM
    tiles_per_seq = seq // TM
    last = n_tiles - 1
    tri = (jnp.arange(TM)[None, :] >= jnp.arange(TM)[:, None]).astype(BF16)
    decay, qdec, kdec, gl = _retention_tables()
    gain = ret_out_gain.reshape(RET_HEADS, 1, RET_DIM)

    cur = lambda i: jnp.minimum(i, last)
    row_cur = lambda i: (cur(i), 0)
    row_prev_tile = lambda i: (jnp.maximum(cur(i) - 1, 0), 0)
    row_lag = lambda i: (jnp.maximum(i - 1, 0), 0)
    t_cur = lambda i: (cur(i), 0, 0)
    t_prev = lambda i: (jnp.maximum(cur(i) - 1, 0), 0, 0)
    full2 = lambda i: (0, 0)
    full3 = lambda i: (0, 0, 0)

    rows = lambda imap: pl.BlockSpec((TM, D_MODEL), imap)
    trans = lambda imap: pl.BlockSpec((None, D_MODEL, TM), imap)
    table = lambda a: pl.BlockSpec(a.shape, full3)
    weight = pl.BlockSpec((D_MODEL, D_MODEL), full2)
    hbm = pl.BlockSpec(memory_space=pl.ANY)
    return pl.pallas_call(
        functools.partial(_bm_kernel, tiles_per_seq=tiles_per_seq),
        grid=(n_tiles + 1,),
        in_specs=[
            trans(t_cur), rows(row_cur), rows(row_prev_tile), trans(t_cur), trans(t_prev),
            rows(row_cur),
            rows(row_cur), rows(row_cur), rows(row_cur), rows(row_cur),
            table(decay), table(qdec), table(kdec), table(gl), table(gain),
            rows(row_cur), pl.BlockSpec((1, D_MODEL), full2),
            pl.BlockSpec((D_MODEL, 2 * D_MODEL), lambda i: (0, IN_SEGMENTS // 2),
                         pipeline_mode=pl.Buffered(1)),
            pl.BlockSpec((1, 2 * D_MODEL), full2),
            rows(row_lag), weight, weight, weight,
            pl.BlockSpec((TM, TM), full2),
            hbm, hbm,
        ],
        out_specs=rows(row_lag),
        out_shape=jax.ShapeDtypeStruct((m, D_MODEL), F32),
        scratch_shapes=[
            pltpu.VMEM((2, TM, D_MODEL), BF16),
            pltpu.VMEM((2, TM, D_MODEL), BF16),
            pltpu.VMEM((2, TM, 2 * D_MODEL), F32),
            pltpu.VMEM((TM, D_MODEL), BF16),
            pltpu.VMEM((TM, D_MODEL), BF16),
            pltpu.VMEM((RET_HEADS, RET_DIM, RET_DIM), F32),
            pltpu.VMEM((RET_HEADS, TM, TM), F32),
            pltpu.VMEM((RET_HEADS, TM, RET_DIM), F32),
            pltpu.VMEM((2 * SB_HEADS, TM, TM), F32),
            pltpu.VMEM((2 * SB_HEADS, TM, TM), BF16),
            pltpu.VMEM((SB_HEADS, SB_HEAD_DIM, TM), F32),
            pltpu.VMEM((SB_HEADS, 1, TM), F32),
            pltpu.VMEM((TM, D_MODEL), BF16),
            pltpu.VMEM((D_MODEL, TM), BF16),
        ],
        compiler_params=pltpu.CompilerParams(
            dimension_semantics=("arbitrary",),
            vmem_limit_bytes=VMEM_LIMIT_BYTES),
        name="branches_merge",
    )(qt, k, k, vt, vt, sg, rq, rk, rv, rg, decay, qdec, kdec, gl, gain,
      x2, norm_gain, w_gate, b_gate, x2, wsb, wret, wout, tri, k, vt)


def _rope_tables(seq, tile):
    d = RET_DIM
    inv_freq = ROPE_BASE ** (-np.arange(0, d, 2, dtype=np.float64) / d)
    base = (np.arange(seq // tile, dtype=np.float64) * tile)[:, None, None] * inv_freq
    local = np.arange(tile, dtype=np.float64)[:, None] * inv_freq
    return tuple(jnp.asarray(a, F32)
                 for a in (np.cos(base), np.sin(base), np.cos(local), np.sin(local)))


def kernel(x, norm_gain, w_in, b_merge, sb_q_gain, sb_k_gain, ret_out_gain,
           w_branch_sb, w_branch_ret, w_out):
    batch, seq, d_model = x.shape
    depth = norm_gain.shape[0]
    assert d_model == D_MODEL and w_in.shape[-1] == N_SEGMENTS * D_MODEL
    assert seq % (IN_TILES * TM) == 0 and TM % CHUNK == 0
    x2 = x.reshape(batch * seq, D_MODEL)
    for layer in range(depth):
        gain = norm_gain[layer][None, :]
        w_in_bf = w_in[layer].astype(BF16)
        qt, k, vt, sg, rq, rk, rv, rg = _inproj(
            x2, gain, w_in_bf, sb_q_gain[layer][None, :], sb_k_gain[layer][None, :], seq)
        x2 = _branches_merge(
            qt, k, vt, sg, rq, rk, rv, rg, x2, gain,
            w_in_bf, b_merge[layer].reshape(1, 2 * D_MODEL),
            ret_out_gain[layer],
            w_branch_sb[layer].astype(BF16), w_branch_ret[layer].astype(BF16),
            w_out[layer].astype(BF16), seq)
    return x2.reshape(batch, seq, D_MODEL)
```

```python
import functools

import numpy as np
import jax
import jax.numpy as jnp
from jax import lax
from jax.experimental import pallas as pl
from jax.experimental.pallas import tpu as pltpu

D_MODEL = 1024
SB_HEADS = 8
SB_HEAD_DIM = D_MODEL // SB_HEADS
RET_HEADS = 4
RET_DIM = D_MODEL // RET_HEADS
CHUNK = 64
ROPE_BASE = 10000.0
EPS = 1e-6
N_SEGMENTS = 10
IN_SEGMENTS = 8
LOG2E = 1.4426950408889634

VMEM_LIMIT_BYTES = 56 * 1024 * 1024

TM = 256
IN_TILES = 2
SB_DONE_LOG2 = 160.0

F32 = jnp.float32
BF16 = jnp.bfloat16


def _rms(x, gain):
    y = x * lax.rsqrt(jnp.mean(x * x, axis=-1, keepdims=True) + EPS)
    return y * gain


def _silu(g):
    return g * jax.nn.sigmoid(g)


def _inproj_kernel(x_ref, ng_ref, w_ref, qg_ref, kg_ref, cosa_ref, sina_ref, cosb_ref,
                   sinb_ref, qt_ref, k_ref, vt_ref, *row_out_refs):
    for sub in range(IN_TILES):
        rows = pl.ds(sub * TM, TM)
        _inproj_tile(x_ref.at[rows], ng_ref, w_ref, qg_ref, kg_ref, cosa_ref.at[sub],
                     sina_ref.at[sub], cosb_ref, sinb_ref, qt_ref.at[sub], k_ref.at[rows],
                     vt_ref.at[sub], *[r.at[rows] for r in row_out_refs])


def _inproj_tile(x_ref, ng_ref, w_ref, qg_ref, kg_ref, cosa_ref, sina_ref, cosb_ref,
                 sinb_ref, qt_ref, k_ref, vt_ref, sg_ref, rq_ref, rk_ref, rv_ref, rg_ref):
    hb = _rms(x_ref[...], ng_ref[...]).astype(BF16)

    def seg(s):
        return jnp.dot(hb, w_ref[:, s * D_MODEL:(s + 1) * D_MODEL],
                       preferred_element_type=F32)

    def heads(p, out_ref, gain_ref=None, post_scale=None, transpose=False):
        for h in range(SB_HEADS):
            sl = slice(h * SB_HEAD_DIM, (h + 1) * SB_HEAD_DIM)
            y = p[:, sl]
            if gain_ref is not None:
                y = _rms(y, gain_ref[...])
            if post_scale is not None:
                y = y * post_scale
            if transpose:
                out_ref[sl, :] = y.T.astype(BF16)
            else:
                out_ref[:, sl] = y.astype(BF16)

    ca, sa = cosa_ref[...], sina_ref[...]
    cb, sb = cosb_ref[...], sinb_ref[...]
    cos = ca * cb - sa * sb
    sin = sa * cb + ca * sb

    def rotary(p, out_ref, post_scale):
        half = RET_DIM // 2
        for h in range(RET_HEADS):
            t1 = p[:, h * RET_DIM:h * RET_DIM + half]
            t2 = p[:, h * RET_DIM + half:(h + 1) * RET_DIM]
            o1 = t1 * cos - t2 * sin
            o2 = t1 * sin + t2 * cos
            if post_scale is not None:
                o1 = o1 * post_scale
                o2 = o2 * post_scale
            out_ref[:, h * RET_DIM:h * RET_DIM + half] = o1.astype(BF16)
            out_ref[:, h * RET_DIM + half:(h + 1) * RET_DIM] = o2.astype(BF16)

    heads(seg(0), qt_ref, qg_ref, (SB_HEAD_DIM ** -0.5) * LOG2E, transpose=True)
    heads(seg(1), k_ref, kg_ref)
    heads(seg(2), vt_ref, transpose=True)
    sg_ref[...] = _silu(seg(3))
    rotary(seg(4), rq_ref, None)
    rotary(seg(5), rk_ref, RET_DIM ** -0.5)
    rg_ref[...] = _silu(seg(7))
    rv_ref[...] = seg(6).astype(BF16)


def _inproj(x2, norm_gain, w_in_bf, q_gain, k_gain, seq):
    m = x2.shape[0]
    rows = IN_TILES * TM
    steps_per_seq = seq // rows
    cosa, sina, cosb, sinb = _rope_tables(seq, TM)
    row = lambda i: (i, 0)
    const = lambda i: (0, 0)
    tile_base = pl.BlockSpec((IN_TILES, 1, RET_DIM // 2),
                             lambda i: (i % steps_per_seq, 0, 0))
    bf_out = jax.ShapeDtypeStruct((m, D_MODEL), BF16)
    f32_out = jax.ShapeDtypeStruct((m, D_MODEL), F32)
    out_spec = pl.BlockSpec((rows, D_MODEL), row)
    t_out = jax.ShapeDtypeStruct((m // TM, D_MODEL, TM), BF16)
    t_spec = pl.BlockSpec((IN_TILES, D_MODEL, TM), lambda i: (i, 0, 0))
    return pl.pallas_call(
        _inproj_kernel,
        grid=(m // rows,),
        in_specs=[
            pl.BlockSpec((rows, D_MODEL), row),
            pl.BlockSpec((1, D_MODEL), const),
            pl.BlockSpec((D_MODEL, IN_SEGMENTS * D_MODEL), const,
                         pipeline_mode=pl.Buffered(1)),
            pl.BlockSpec((1, SB_HEAD_DIM), const),
            pl.BlockSpec((1, SB_HEAD_DIM), const),
            tile_base, tile_base,
            pl.BlockSpec((TM, RET_DIM // 2), const),
            pl.BlockSpec((TM, RET_DIM // 2), const),
        ],
        out_specs=[t_spec, out_spec, t_spec] + [out_spec] * 5,
        out_shape=[t_out, bf_out, t_out, f32_out, bf_out, bf_out, bf_out, f32_out],
        compiler_params=pltpu.CompilerParams(
            dimension_semantics=("arbitrary",),
            vmem_limit_bytes=VMEM_LIMIT_BYTES),
        name="inproj",
    )(x2, norm_gain, w_in_bf, q_gain, k_gain, cosa, sina, cosb, sinb)


def _softplus2(z2):
    return jnp.maximum(z2, 0.0) + jnp.log(1.0 + jnp.exp2(-jnp.abs(z2))) * LOG2E


def _bm_kernel(qt_ref, kc_ref, kp_ref, vtc_ref, vtp_ref, sg_ref,
               rq_ref, rk_ref, rv_ref, rg_ref, dec_ref, qd_ref, kd_ref, gl_ref, og_ref,
               xc_ref, ng_ref, wm_ref, bm_ref, x_ref, wsb_ref, wret_ref, wout_ref, tri_ref,
               k_hbm, vt_hbm,
               o_ref,
               oa_ref, ob_ref, gate_ref, hb_ref, mg_ref, state_ref, rs_ref, rq_state_ref,
               z_ref, sp_ref, acc_ref, carry_ref, kbuf_ref, vbuf_ref, *, tiles_per_seq):
    m = pl.program_id(0)
    cur = jnp.minimum(m, pl.num_programs(0) - 2)
    t = cur % tiles_per_seq
    has_prev = t > 0
    slot = m % 2

    @pl.when(m == 0)
    def _():
        def zero_rows(r, _):
            rows = pl.ds(pl.multiple_of(r * 16, 16), 16)
            oa_ref[1, rows, :] = jnp.zeros((16, D_MODEL), BF16)
            ob_ref[1, rows, :] = jnp.zeros((16, D_MODEL), BF16)
            gate_ref[1, rows, :] = jnp.zeros((16, 2 * D_MODEL), F32)
            return 0

        lax.fori_loop(0, TM // 16, zero_rows, 0)

    def head_cols(h):
        return slice(h * SB_HEAD_DIM, (h + 1) * SB_HEAD_DIM)

    def causal_mask():
        row = lax.broadcasted_iota(jnp.int32, (TM, TM), 0)
        col = lax.broadcasted_iota(jnp.int32, (TM, TM), 1)
        return row < col

    def write_head(h):
        oa_ref[slot, :, head_cols(h)] = (
            acc_ref[h].T * sg_ref[:, head_cols(h)]).astype(BF16)

    n_piece = D_MODEL // TM

    def gate_piece(j):
        cols = slice(j * TM, (j + 1) * TM)
        g = jnp.dot(hb_ref[...], wm_ref[:, cols], preferred_element_type=F32)
        gate_ref[slot, :, cols] = jax.nn.sigmoid(g + bm_ref[:, cols])

    def merged_piece(j):
        cols = slice(j * TM, (j + 1) * TM)
        gcols = slice(D_MODEL + j * TM, D_MODEL + (j + 1) * TM)
        pa = jnp.dot(oa_ref[1 - slot], wsb_ref[:, cols], preferred_element_type=F32)
        pb = jnp.dot(ob_ref[1 - slot], wret_ref[:, cols], preferred_element_type=F32)
        mg_ref[:, cols] = (gate_ref[1 - slot, :, cols] * pa
                           + gate_ref[1 - slot, :, gcols] * pb).astype(BF16)

    def out_piece(j):
        cols = slice(j * TM, (j + 1) * TM)
        o_ref[:, cols] = x_ref[:, cols] + jnp.dot(mg_ref[...], wout_ref[:, cols],
                                                  preferred_element_type=F32)

    def ret_cols(h):
        return slice(h * RET_DIM, (h + 1) * RET_DIM)

    def retention_scores(h):
        q = rq_ref[:, ret_cols(h)]
        k = rk_ref[:, ret_cols(h)]
        v = rv_ref[:, ret_cols(h)]
        rs_ref[h] = lax.dot_general(q, k, (((1,), (1,)), ((), ())),
                                    preferred_element_type=F32)
        state = jnp.where(has_prev, state_ref[h], 0.0)
        rq_state_ref[h] = jnp.dot(q, state.astype(BF16), preferred_element_type=F32)
        kdt = (k.astype(F32) * kd_ref[h]).T.astype(BF16)
        state_ref[h] = state * gl_ref[h] + jnp.dot(kdt, v, preferred_element_type=F32)

    def retention_out(h):
        p = (rs_ref[h] * dec_ref[h]).astype(BF16)
        o = jnp.dot(p, rv_ref[:, ret_cols(h)], preferred_element_type=F32)
        o = o + qd_ref[h] * rq_state_ref[h]
        ob_ref[slot, :, ret_cols(h)] = (
            _rms(o, og_ref[h]) * rg_ref[:, ret_cols(h)]).astype(BF16)

    units = [(h, d) for d in (0, 1) for h in range(SB_HEADS)]
    n_units = len(units)
    n_gate = 2 * n_piece
    for r in range(0, TM, 64):
        hb_ref[r:r + 64, :] = _rms(xc_ref[r:r + 64, :], ng_ref[...]).astype(BF16)
    for u, (h, d) in enumerate(units):
        kt = (kc_ref if d == 0 else kp_ref)[:, head_cols(h)]
        z_ref[u] = jnp.dot(kt, qt_ref[head_cols(h), :], preferred_element_type=F32)
        if (u + 1) % (n_units // RET_HEADS) == 0:
            retention_scores(u // (n_units // RET_HEADS))
        if (u + 3) % (n_units // (n_gate // 2)) == 0:
            gate_piece(u // (n_units // (n_gate // 2)))
    for u, (h, d) in enumerate(units):
        if u % (n_units // n_piece) == 0:
            merged_piece(u // (n_units // n_piece))
        sp2 = _softplus2(z_ref[u])
        sp2 = jnp.where(causal_mask() if d == 0 else has_prev, sp2, 0.0)
        sp_ref[u] = sp2.astype(BF16)
    total = {}

    def cumsum(h):
        for u in (h, SB_HEADS + h):
            cs = jnp.dot(tri_ref[...], sp_ref[u], preferred_element_type=F32)
            z_ref[u] = z_ref[u] - cs
            total[u] = cs[0:1, :]

    ahead = 4
    for h in range(ahead):
        cumsum(h)
    for h in range(SB_HEADS):
        if h + ahead < SB_HEADS:
            cumsum(h + ahead)
        if h % (SB_HEADS // RET_HEADS) == 0:
            retention_out(h // (SB_HEADS // RET_HEADS))
        else:
            gate_piece(n_gate // 2 + h // (SB_HEADS // RET_HEADS))
        w = jnp.where(causal_mask(), jnp.exp2(z_ref[h]), 0.0)
        carry = total[h]
        pv = jnp.dot(vtc_ref[head_cols(h), :], w.astype(BF16), preferred_element_type=F32)
        w = jnp.where(has_prev, jnp.exp2(z_ref[SB_HEADS + h] - carry), 0.0)
        pv = pv + jnp.dot(vtp_ref[head_cols(h), :], w.astype(BF16),
                          preferred_element_type=F32)
        acc_ref[h] = pv
        carry_ref[h] = carry + total[SB_HEADS + h]
        write_head(h)
        if (h + 1) % (SB_HEADS // n_piece) == 0:
            out_piece(h // (SB_HEADS // n_piece))

    def unfinished():
        return jnp.min(carry_ref[...]) < SB_DONE_LOG2

    @pl.when(jnp.logical_and(t >= 2, unfinished()))
    def _():
        def more(state):
            d, go = state
            return jnp.logical_and(d <= t, go)

        def sweep(state):
            d, _ = state
            tile = cur - d
            row0 = pl.multiple_of(tile * TM, TM)
            pltpu.sync_copy(k_hbm.at[pl.ds(row0, TM), :], kbuf_ref)
            pltpu.sync_copy(vt_hbm.at[tile], vbuf_ref)
            for h in range(SB_HEADS):
                z2 = jnp.dot(kbuf_ref[:, head_cols(h)], qt_ref[head_cols(h), :],
                             preferred_element_type=F32)
                cs = jnp.dot(tri_ref[...], _softplus2(z2).astype(BF16),
                             preferred_element_type=F32)
                carry = carry_ref[h]
                w = jnp.exp2(z2 - cs - carry)
                carry_ref[h] = carry + cs[0:1, :]
                acc_ref[h] += jnp.dot(vbuf_ref[head_cols(h), :], w.astype(BF16),
                                      preferred_element_type=F32)
            return d + 1, unfinished()

        lax.while_loop(more, sweep, (jnp.int32(2), True))
        for h in range(SB_HEADS):
            write_head(h)


def _retention_tables():
    log_gamma = np.log1p(-np.exp2(-5.0 - np.arange(RET_HEADS)))
    t = np.arange(TM, dtype=np.float64)
    ct = np.arange(TM) // CHUNK
    dist = t[:, None] - t[None, :]
    lg = log_gamma[:, None, None]
    same = (ct[:, None] == ct[None, :])[None]
    earlier = (ct[None, :] < ct[:, None])[None]
    decay = np.where(same, np.exp(lg * np.abs(dist)[None]),
                     np.where(earlier, np.exp(lg * dist[None]), 0.0))
    ones = np.ones((1, 1, RET_DIM))
    qdec = np.exp(log_gamma[:, None] * (t + 1.0)[None, :])[:, :, None] * ones
    kdec = np.exp(log_gamma[:, None] * (TM - 1.0 - t)[None, :])[:, :, None] * ones
    gl = np.exp(log_gamma * TM)[:, None, None] * ones
    return tuple(jnp.asarray(a, F32) for a in (decay, qdec, kdec, gl))


def _branches_merge(qt, k, vt, sg, rq, rk, rv, rg, x2, norm_gain, w_gate, b_gate,
                    ret_out_gain, wsb, wret, wout, seq):
    m = x2.shape[0]
    n_tiles = m // TM
    tiles_per_seq = seq // TM
    last = n_tiles - 1
    tri = (jnp.arange(TM)[None, :] >= jnp.arange(TM)[:, None]).astype(BF16)
    decay, qdec, kdec, gl = _retention_tables()
    gain = ret_out_gain.reshape(RET_HEADS, 1, RET_DIM)

    cur = lambda i: jnp.minimum(i, last)
    row_cur = lambda i: (cur(i), 0)
    row_prev_tile = lambda i: (jnp.maximum(cur(i) - 1, 0), 0)
    row_lag = lambda i: (jnp.maximum(i - 1, 0), 0)
    t_cur = lambda i: (cur(i), 0, 0)
    t_prev = lambda i: (jnp.maximum(cur(i) - 1, 0), 0, 0)
    full2 = lambda i: (0, 0)
    full3 = lambda i: (0, 0, 0)

    rows = lambda imap: pl.BlockSpec((TM, D_MODEL), imap)
    trans = lambda imap: pl.BlockSpec((None, D_MODEL, TM), imap)
    table = lambda a: pl.BlockSpec(a.shape, full3)
    weight = pl.BlockSpec((D_MODEL, D_MODEL), full2)
    hbm = pl.BlockSpec(memory_space=pl.ANY)
    return pl.pallas_call(
        functools.partial(_bm_kernel, tiles_per_seq=tiles_per_seq),
        grid=(n_tiles + 1,),
        in_specs=[
            trans(t_cur), rows(row_cur), rows(row_prev_tile), trans(t_cur), trans(t_prev),
            rows(row_cur),
            rows(row_cur), rows(row_cur), rows(row_cur), rows(row_cur),
            table(decay), table(qdec), table(kdec), table(gl), table(gain),
            rows(row_cur), pl.BlockSpec((1, D_MODEL), full2),
            pl.BlockSpec((D_MODEL, 2 * D_MODEL), lambda i: (0, IN_SEGMENTS // 2),
                         pipeline_mode=pl.Buffered(1)),
            pl.BlockSpec((1, 2 * D_MODEL), full2),
            rows(row_lag), weight, weight, weight,
            pl.BlockSpec((TM, TM), full2),
            hbm, hbm,
        ],
        out_specs=rows(row_lag),
        out_shape=jax.ShapeDtypeStruct((m, D_MODEL), F32),
        scratch_shapes=[
            pltpu.VMEM((2, TM, D_MODEL), BF16),
            pltpu.VMEM((2, TM, D_MODEL), BF16),
            pltpu.VMEM((2, TM, 2 * D_MODEL), F32),
            pltpu.VMEM((TM, D_MODEL), BF16),
            pltpu.VMEM((TM, D_MODEL), BF16),
            pltpu.VMEM((RET_HEADS, RET_DIM, RET_DIM), F32),
            pltpu.VMEM((RET_HEADS, TM, TM), F32),
            pltpu.VMEM((RET_HEADS, TM, RET_DIM), F32),
            pltpu.VMEM((2 * SB_HEADS, TM, TM), F32),
            pltpu.VMEM((2 * SB_HEADS, TM, TM), BF16),
            pltpu.VMEM((SB_HEADS, SB_HEAD_DIM, TM), F32),
            pltpu.VMEM((SB_HEADS, 1, TM), F32),
            pltpu.VMEM((TM, D_MODEL), BF16),
            pltpu.VMEM((D_MODEL, TM), BF16),
        ],
        compiler_params=pltpu.CompilerParams(
            dimension_semantics=("arbitrary",),
            vmem_limit_bytes=VMEM_LIMIT_BYTES),
        name="branches_merge",
    )(qt, k, k, vt, vt, sg, rq, rk, rv, rg, decay, qdec, kdec, gl, gain,
      x2, norm_gain, w_gate, b_gate, x2, wsb, wret, wout, tri, k, vt)


def _rope_tables(seq, tile):
    d = RET_DIM
    inv_freq = ROPE_BASE ** (-np.arange(0, d, 2, dtype=np.float64) / d)
    base = (np.arange(seq // tile, dtype=np.float64) * tile)[:, None, None] * inv_freq
    local = np.arange(tile, dtype=np.float64)[:, None] * inv_freq
    return tuple(jnp.asarray(a, F32)
                 for a in (np.cos(base), np.sin(base), np.cos(local), np.sin(local)))


def kernel(x, norm_gain, w_in, b_merge, sb_q_gain, sb_k_gain, ret_out_gain,
           w_branch_sb, w_branch_ret, w_out):
    batch, seq, d_model = x.shape
    depth = norm_gain.shape[0]
    assert d_model == D_MODEL and w_in.shape[-1] == N_SEGMENTS * D_MODEL
    assert seq % (IN_TILES * TM) == 0 and TM % CHUNK == 0
    x2 = x.reshape(batch * seq, D_MODEL)
    for layer in range(depth):
        gain = norm_gain[layer][None, :]
        w_in_bf = w_in[layer].astype(BF16)
        qt, k, vt, sg, rq, rk, rv, rg = _inproj(
            x2, gain, w_in_bf, sb_q_gain[layer][None, :], sb_k_gain[layer][None, :], seq)
        x2 = _branches_merge(
            qt, k, vt, sg, rq, rk, rv, rg, x2, gain,
            w_in_bf, b_merge[layer].reshape(1, 2 * D_MODEL),
            ret_out_gain[layer],
            w_branch_sb[layer].astype(BF16), w_branch_ret[layer].astype(BF16),
            w_out[layer].astype(BF16), seq)
    return x2.reshape(batch, seq, D_MODEL)
```
